```python
import math
import jax
import jax.numpy as jnp
from jax import lax
import numpy as np

D_MODEL = 1024
BATCH = 4
SEQ = 4096
DEPTH = 1

CHUNK = 64
Q_BLOCK = 128
EPS = 1e-6
NEG_INF = -1e30

N_ATTN_HEADS = 8
HEAD_DIM = D_MODEL // (2 * N_ATTN_HEADS)
V_HEAD_DIM = 2 * HEAD_DIM
QK_WIDTH = N_ATTN_HEADS * 2 * HEAD_DIM
ATTN_WIDTH = N_ATTN_HEADS * V_HEAD_DIM

D_RNN = (5 * D_MODEL) // 4
RNN_BLOCK = 64
N_RNN_BLOCKS = D_RNN // RNN_BLOCK
CONV_WIDTH = 4
RG_C = 8.0

N_BRANCH = 2
IN_SPLITS = (QK_WIDTH, 2 * QK_WIDTH, 2 * QK_WIDTH + ATTN_WIDTH,
             2 * QK_WIDTH + ATTN_WIDTH + D_RNN, 2 * QK_WIDTH + ATTN_WIDTH + 2 * D_RNN)
IN_COLS = 2 * QK_WIDTH + ATTN_WIDTH + 2 * D_RNN + N_BRANCH * D_MODEL

PEER_HEADS = 8
PEER_N_KEYS = 128
PEER_N_EXPERTS = PEER_N_KEYS ** 2
PEER_QDIM = 256
PEER_HALF = PEER_QDIM // 2
PEER_TOPK = 16
PEER_BLOCK = 128

kernel_name = 'hybrid_diffattn_rglru_peer_block'


def rms_norm(x, g):
    xf = x.astype(jnp.float32)
    y = xf * lax.rsqrt(jnp.mean(xf * xf, axis=-1, keepdims=True) + EPS)
    return (y * g.astype(jnp.float32)).astype(x.dtype)


def diff_attention(q, k, v, q_norm_g, k_norm_g, lambda_q1, lambda_k1, lambda_q2, lambda_k2,
                   subln_g, lam_init):
    b, s = q.shape[0], q.shape[1]
    n_blk = s // Q_BLOCK
    q = rms_norm(q, q_norm_g)
    k = rms_norm(k, k_norm_g)
    lam = (jnp.exp(jnp.sum(lambda_q1.astype(jnp.float32) * lambda_k1.astype(jnp.float32)))
           - jnp.exp(jnp.sum(lambda_q2.astype(jnp.float32) * lambda_k2.astype(jnp.float32)))
           + lam_init)
    scale = HEAD_DIM ** -0.5
    key_chunk = jnp.arange(s) // CHUNK
    q_blocks = q.reshape(b, n_blk, Q_BLOCK, N_ATTN_HEADS, 2, HEAD_DIM).transpose(1, 0, 2, 3, 4, 5)

    def one_block(args):
        q_blk, blk = args
        scores = jnp.einsum('bqhcd,bkhcd->bhcqk', q_blk, k).astype(jnp.float32) * scale
        q_chunk = (blk * Q_BLOCK + jnp.arange(Q_BLOCK)) // CHUNK
        visible = key_chunk[None, :] <= q_chunk[:, None]
        scores = jnp.where(visible, scores, NEG_INF)
        probs = jax.nn.softmax(scores, axis=-1)
        diff = probs[:, :, 0] - lam * probs[:, :, 1]
        return jnp.einsum('bhqk,bkhd->bqhd', diff.astype(v.dtype), v)

    o = lax.map(one_block, (q_blocks, jnp.arange(n_blk)))
    o = o.transpose(1, 0, 2, 3, 4).reshape(b, s, N_ATTN_HEADS, V_HEAD_DIM)
    o = rms_norm(o, subln_g) * (1.0 - lam_init)
    return o.reshape(b, s, ATTN_WIDTH)


def causal_depthwise_conv(x, w, bias):
    s = x.shape[1]
    xp = jnp.pad(x, ((0, 0), (CONV_WIDTH - 1, 0), (0, 0)))
    y = bias
    for tap in range(CONV_WIDTH):
        y = y + xp[:, tap:tap + s] * w[tap]
    return y


def rg_lru(x, w_a, b_a, w_x, b_x, rg_lambda):
    b, s, _ = x.shape
    xb = x.reshape(b, s, N_RNN_BLOCKS, RNN_BLOCK)
    gate_r = jax.nn.sigmoid(jnp.einsum('bsgi,gij->bsgj', xb, w_a).reshape(b, s, D_RNN) + b_a)
    gate_i = jax.nn.sigmoid(jnp.einsum('bsgi,gij->bsgj', xb, w_x).reshape(b, s, D_RNN) + b_x)
    log_a = -RG_C * gate_r.astype(jnp.float32) * jax.nn.softplus(-rg_lambda.astype(jnp.float32))
    a = jnp.exp(log_a)
    mult = jnp.sqrt(-jnp.expm1(2.0 * log_a))
    u = mult * (gate_i * x).astype(jnp.float32)

    def combine(left, right):
        a_l, u_l = left
        a_r, u_r = right
        return a_l * a_r, a_r * u_l + u_r

    _, h = lax.associative_scan(combine, (a, u), axis=1)
    return h.astype(x.dtype)


def peer_ffn(h, w_q, sub_keys, u_tab, v_tab):
    b, s, d = h.shape
    tokens = h.reshape((b * s) // PEER_BLOCK, PEER_BLOCK, d)

    def one_block(hb):
        q = (hb @ w_q).reshape(PEER_BLOCK, PEER_HEADS, 2, PEER_HALF)
        sc = jnp.einsum('thcd,hcnd->thcn', q, sub_keys).astype(jnp.float32)
        s1, i1 = lax.top_k(sc[:, :, 0], PEER_TOPK)
        s2, i2 = lax.top_k(sc[:, :, 1], PEER_TOPK)
        cand_s = (s1[..., :, None] + s2[..., None, :]).reshape(PEER_BLOCK, PEER_HEADS, PEER_TOPK * PEER_TOPK)
        cand_i = (i1[..., :, None] * PEER_N_KEYS + i2[..., None, :]).reshape(PEER_BLOCK, PEER_HEADS, PEER_TOPK * PEER_TOPK)
        top_s, pos = lax.top_k(cand_s, PEER_TOPK)
        idx = jnp.take_along_axis(cand_i, pos, axis=-1)
        g = jax.nn.softmax(top_s, axis=-1)
        u = u_tab[idx]
        act = jax.nn.gelu(jnp.einsum('thkd,td->thk', u, hb))
        wts = (g * act.astype(jnp.float32)).astype(hb.dtype)
        v = v_tab[idx]
        return jnp.einsum('thk,thkd->td', wts, v)

    out = lax.map(one_block, tokens)
    return out.reshape(b, s, d)


def hybrid_block(x, lam_init, norm1_g, w_in, b_gate, q_norm_g, k_norm_g, lambda_q1, lambda_k1,
                 lambda_q2, lambda_k2, subln_g, conv_w, conv_b, w_rg_a, b_rg_a, w_rg_x, b_rg_x,
                 rg_lambda, w_br_attn, w_br_rnn, w_out, norm2_g, w_peer_q, peer_sub_keys,
                 peer_u, peer_v):
    b, s, _ = x.shape
    h = rms_norm(x, norm1_g)
    z = h @ w_in
    q, k, v, xr, yr, gate_logits = jnp.split(z, IN_SPLITS, axis=-1)
    q = q.reshape(b, s, N_ATTN_HEADS, 2, HEAD_DIM)
    k = k.reshape(b, s, N_ATTN_HEADS, 2, HEAD_DIM)
    v = v.reshape(b, s, N_ATTN_HEADS, V_HEAD_DIM)
    attn = diff_attention(q, k, v, q_norm_g, k_norm_g, lambda_q1, lambda_k1, lambda_q2,
                          lambda_k2, subln_g, lam_init)
    rnn = rg_lru(causal_depthwise_conv(xr, conv_w, conv_b), w_rg_a, b_rg_a, w_rg_x, b_rg_x,
                 rg_lambda) * jax.nn.gelu(yr)
    gates = jax.nn.sigmoid(gate_logits + b_gate).reshape(b, s, N_BRANCH, D_MODEL)
    merged = gates[:, :, 0] * (attn @ w_br_attn) + gates[:, :, 1] * (rnn @ w_br_rnn)
    x = x + merged @ w_out
    x = x + peer_ffn(rms_norm(x, norm2_g), w_peer_q, peer_sub_keys, peer_u, peer_v)
    return x


def setup_inputs(seed: int = 0) -> dict:
    key = jax.random.key(seed)
    ks = jax.random.split(key, 26)
    L = DEPTH

    def nrm(k, shape, scale):
        return jax.random.normal(k, shape, jnp.float32) * scale

    a0 = jax.random.uniform(ks[17], (L, D_RNN), jnp.float32, 0.9, 0.999)
    return {
        'x': nrm(ks[0], (BATCH, SEQ, D_MODEL), 1.0),
        'norm1_g': 1.0 + nrm(ks[1], (L, D_MODEL), 0.02),
        'w_in': nrm(ks[2], (L, D_MODEL, IN_COLS), D_MODEL ** -0.5),
        'b_gate': nrm(ks[3], (L, N_BRANCH * D_MODEL), 0.02),
        'q_norm_g': 1.0 + nrm(ks[4], (L, HEAD_DIM), 0.02),
        'k_norm_g': 1.0 + nrm(ks[5], (L, HEAD_DIM), 0.02),
        'lambda_q1': nrm(ks[6], (L, HEAD_DIM), 0.1),
        'lambda_k1': nrm(ks[7], (L, HEAD_DIM), 0.1),
        'lambda_q2': nrm(ks[8], (L, HEAD_DIM), 0.1),
        'lambda_k2': nrm(ks[9], (L, HEAD_DIM), 0.1),
        'subln_g': 1.0 + nrm(ks[10], (L, V_HEAD_DIM), 0.02),
        'conv_w': nrm(ks[11], (L, CONV_WIDTH, D_RNN), CONV_WIDTH ** -0.5),
        'conv_b': nrm(ks[12], (L, D_RNN), 0.02),
        'w_rg_a': nrm(ks[13], (L, N_RNN_BLOCKS, RNN_BLOCK, RNN_BLOCK), RNN_BLOCK ** -0.5),
        'b_rg_a': nrm(ks[14], (L, D_RNN), 0.02),
        'w_rg_x': nrm(ks[15], (L, N_RNN_BLOCKS, RNN_BLOCK, RNN_BLOCK), RNN_BLOCK ** -0.5),
        'b_rg_x': nrm(ks[16], (L, D_RNN), 0.02),
        'rg_lambda': jnp.log(a0) - jnp.log1p(-a0),
        'w_br_attn': nrm(ks[18], (L, ATTN_WIDTH, D_MODEL), ATTN_WIDTH ** -0.5),
        'w_br_rnn': nrm(ks[19], (L, D_RNN, D_MODEL), D_RNN ** -0.5),
        'w_out': nrm(ks[20], (L, D_MODEL, D_MODEL), D_MODEL ** -0.5),
        'norm2_g': 1.0 + nrm(ks[21], (L, D_MODEL), 0.02),
        'w_peer_q': nrm(ks[22], (L, D_MODEL, PEER_HEADS * PEER_QDIM), D_MODEL ** -0.5),
        'peer_sub_keys': nrm(ks[23], (L, PEER_HEADS, 2, PEER_N_KEYS, PEER_HALF), PEER_HALF ** -0.5),
        'peer_u': nrm(ks[24], (L, PEER_N_EXPERTS, D_MODEL), D_MODEL ** -0.5),
        'peer_v': nrm(ks[25], (L, PEER_N_EXPERTS, D_MODEL), PEER_HEADS ** -0.5),
    }


def reference(x, norm1_g, w_in, b_gate, q_norm_g, k_norm_g, lambda_q1, lambda_k1, lambda_q2,
              lambda_k2, subln_g, conv_w, conv_b, w_rg_a, b_rg_a, w_rg_x, b_rg_x, rg_lambda,
              w_br_attn, w_br_rnn, w_out, norm2_g, w_peer_q, peer_sub_keys, peer_u, peer_v):
    for layer in range(DEPTH):
        lam_init = 0.8 - 0.6 * math.exp(-0.3 * layer)
        x = hybrid_block(x, lam_init, norm1_g[layer], w_in[layer], b_gate[layer], q_norm_g[layer],
                         k_norm_g[layer], lambda_q1[layer], lambda_k1[layer], lambda_q2[layer],
                         lambda_k2[layer], subln_g[layer], conv_w[layer], conv_b[layer],
                         w_rg_a[layer], b_rg_a[layer], w_rg_x[layer], b_rg_x[layer],
                         rg_lambda[layer], w_br_attn[layer], w_br_rnn[layer], w_out[layer],
                         norm2_g[layer], w_peer_q[layer], peer_sub_keys[layer], peer_u[layer],
                         peer_v[layer])
    return x
```

```python
import functools
import math

import jax
import jax.numpy as jnp
from jax import lax
from jax.experimental import pallas as pl
from jax.experimental.pallas import tpu as pltpu

F32 = jnp.float32
BF16 = jnp.bfloat16

EPS = 1e-6
NEG_INF = -1e30
CHUNK = 64
N_HEADS = 8
HEAD_DIM = 64
V_HEAD_DIM = 128
RNN_BLOCK = 64
CONV_WIDTH = 4
RG_C = 8.0
PEER_HEADS = 8
PEER_N_KEYS = 128
PEER_TOPK = 16

MXU_TILE = 256
SUBLANES = 8
VMEM_LIMIT = 56 * 1024 * 1024


def _params(*sem):
    return pltpu.CompilerParams(dimension_semantics=sem, vmem_limit_bytes=VMEM_LIMIT)


def _rms(x, g):
    ms = jnp.mean(x * x, axis=-1, keepdims=True)
    return x * lax.rsqrt(ms + EPS) * g


def _dot(a, b):
    return jnp.dot(a, b, preferred_element_type=F32)


def _dot_nt(a, b):
    return lax.dot_general(a, b, (((1,), (1,)), ((), ())), preferred_element_type=F32)


def _qkv_kernel(x_ref, g1_ref, w_ref, gn_ref, gsum_ref, o_ref, h_ref):
    j = pl.program_id(1)

    @pl.when(j == 0)
    def _():
        h_ref[...] = _rms(x_ref[...], g1_ref[...]).astype(BF16)

    y = _dot(h_ref[...], w_ref[...])

    @pl.when(j < 2)
    def _():
        y2 = y * y
        hi = y2.astype(BF16)
        lo = (y2 - hi.astype(F32)).astype(BF16)
        ms = _dot(hi, gsum_ref[...]) + _dot(lo, gsum_ref[...])
        o_ref[...] = (y * lax.rsqrt(ms + EPS) * gn_ref[0]).astype(BF16)

    @pl.when(j == 2)
    def _():
        o_ref[...] = y.astype(BF16)


def _qkv_proj(x2d, g1, w_in_bf, gn, gsum, tm):
    t, d = x2d.shape
    return pl.pallas_call(
        _qkv_kernel,
        out_shape=jax.ShapeDtypeStruct((t, 3 * d), BF16),
        grid=(t // tm, 3),
        in_specs=[
            pl.BlockSpec((tm, d), lambda i, j: (i, 0)),
            pl.BlockSpec((1, d), lambda i, j: (0, 0)),
            pl.BlockSpec((d, d), lambda i, j: (0, j)),
            pl.BlockSpec((1, 1, d), lambda i, j: (j, 0, 0)),
            pl.BlockSpec((d, d), lambda i, j: (0, 0)),
        ],
        out_specs=pl.BlockSpec((tm, d), lambda i, j: (i, j)),
        scratch_shapes=[pltpu.VMEM((tm, d), BF16)],
        compiler_params=_params("parallel", "arbitrary"),
        name="qkv_proj",
    )(x2d, g1, w_in_bf, gn, gsum)


def _attn_kernel(q_ref, k_ref, v_ref, lq1_ref, lk1_ref, lq2_ref, lk2_ref, sg_ref, o_ref,
                 m_ref, l_ref, acc_ref, *, tq, lam_init):
    i = pl.program_id(2)
    q = q_ref[...]
    lane = lax.broadcasted_iota(jnp.int32, q.shape, 1)
    zero = jnp.zeros_like(q)
    qs = jnp.concatenate([jnp.where(lane < HEAD_DIM, q, zero), jnp.where(lane >= HEAD_DIM, q, zero)], axis=0)

    m_ref[...] = jnp.full(m_ref.shape, NEG_INF, F32)
    l_ref[...] = jnp.zeros(l_ref.shape, F32)
    acc_ref[...] = jnp.zeros(acc_ref.shape, F32)

    def step(j, masked):
        start = pl.multiple_of(j * tq, tq)
        k = k_ref[pl.ds(start, tq), :]
        v = v_ref[pl.ds(start, tq), :]
        s = _dot_nt(qs, k)
        if masked:
            r = lax.broadcasted_iota(jnp.int32, s.shape, 0)
            c = lax.broadcasted_iota(jnp.int32, s.shape, 1)
            qpos = jnp.where(r >= tq, r - tq, r)
            s = jnp.where((c // CHUNK) <= (qpos // CHUNK), s, NEG_INF)
        m_prev = m_ref[...]
        m_new = jnp.maximum(m_prev, jnp.max(s, axis=-1, keepdims=True))
        alpha = jnp.exp(m_prev - m_new)
        p = jnp.exp(s - m_new)
        l_ref[...] = alpha * l_ref[...] + jnp.sum(p, axis=-1, keepdims=True)
        acc_ref[...] = alpha * acc_ref[...] + _dot(p.astype(BF16), v)
        m_ref[...] = m_new

    def full_step(j, carry):
        step(j, False)
        return carry

    lax.fori_loop(0, i, full_step, 0)
    step(i, True)

    lam = (jnp.exp(jnp.sum(lq1_ref[...] * lk1_ref[...], axis=-1, keepdims=True))
           - jnp.exp(jnp.sum(lq2_ref[...] * lk2_ref[...], axis=-1, keepdims=True)) + lam_init)
    o = acc_ref[...] / l_ref[...]
    o = o[:tq] - lam * o[tq:]
    o_ref[...] = (_rms(o, sg_ref[...]) * (1.0 - lam_init)).astype(BF16)


def _diff_attention(qkv, lq1, lk1, lq2, lk2, subln_g, batch, seq, lam_init, tq):
    t = qkv.shape[0]
    nq = seq // tq
    hd = V_HEAD_DIM
    small = pl.BlockSpec((1, HEAD_DIM), lambda b, h, i: (0, 0))
    return pl.pallas_call(
        functools.partial(_attn_kernel, tq=tq, lam_init=lam_init),
        out_shape=jax.ShapeDtypeStruct((t, N_HEADS * hd), BF16),
        grid=(batch, N_HEADS, nq),
        in_specs=[
            pl.BlockSpec((tq, hd), lambda b, h, i: (b * nq + i, h)),
            pl.BlockSpec((seq, hd), lambda b, h, i: (b, N_HEADS + h)),
            pl.BlockSpec((seq, hd), lambda b, h, i: (b, 2 * N_HEADS + h)),
            small, small, small, small,
            pl.BlockSpec((1, hd), lambda b, h, i: (0, 0)),
        ],
        out_specs=pl.BlockSpec((tq, hd), lambda b, h, i: (b * nq + i, h)),
        scratch_shapes=[pltpu.VMEM((2 * tq, 1), F32), pltpu.VMEM((2 * tq, 1), F32),
                        pltpu.VMEM((2 * tq, hd), F32)],
        compiler_params=_params("parallel", "parallel", "arbitrary"),
        name="diff_attn",
    )(qkv, qkv, qkv, lq1, lk1, lq2, lk2, subln_g)


def _rglru_kernel(x_ref, g1_ref, wxy_ref, cw_ref, cb_ref, wa_ref, ba_ref, wx_ref, bx_ref, lam_ref, o_ref,
                  xbuf_ref, h_ref, *, ts, d_rnn):
    s_idx = pl.program_id(1)

    @pl.when(s_idx == 0)
    def _():
        xbuf_ref[0:SUBLANES, :] = jnp.zeros((SUBLANES, d_rnn), F32)
        h_ref[...] = jnp.zeros(h_ref.shape, F32)

    h_in = _rms(x_ref[...], g1_ref[...]).astype(BF16)
    xy = _dot(h_in, wxy_ref[...])
    xr = xy[:, :d_rnn]
    yr = xy[:, d_rnn:]

    xbuf_ref[SUBLANES:SUBLANES + ts, :] = xr
    conv = cb_ref[...] + jnp.zeros((ts, d_rnn), F32)
    for tap in range(CONV_WIDTH):
        off = SUBLANES - (CONV_WIDTH - 1) + tap
        conv = conv + xbuf_ref[off:off + ts, :] * cw_ref[tap:tap + 1, :]
    xbuf_ref[0:SUBLANES, :] = xbuf_ref[ts:ts + SUBLANES, :]

    cbf = conv.astype(BF16)
    ga, gi = [], []
    for g in range(d_rnn // MXU_TILE):
        sl = slice(g * MXU_TILE, (g + 1) * MXU_TILE)
        ga.append(_dot(cbf[:, sl], wa_ref[g]))
        gi.append(_dot(cbf[:, sl], wx_ref[g]))
    gate_r = jax.nn.sigmoid(jnp.concatenate(ga, axis=-1) + ba_ref[...])
    gate_i = jax.nn.sigmoid(jnp.concatenate(gi, axis=-1) + bx_ref[...])

    neg_lam = -lam_ref[...]
    softplus = jnp.maximum(neg_lam, 0.0) + jnp.log1p(jnp.exp(-jnp.abs(neg_lam)))
    log_a = -RG_C * gate_r * softplus
    a = jnp.exp(log_a)
    th = jnp.tanh(log_a)
    mult = jnp.sqrt(-2.0 * th / (1.0 - th))
    u = mult * (gate_i * conv)

    row = lax.broadcasted_iota(jnp.int32, (ts, d_rnn), 0)
    d = 1
    while d < ts:
        keep = row >= d
        a_sh = jnp.where(keep, pltpu.roll(a, d, 0), 1.0)
        u_sh = jnp.where(keep, pltpu.roll(u, d, 0), 0.0)
        u = a * u_sh + u
        a = a * a_sh
        d *= 2
    hs = u + a * h_ref[...]
    h_ref[...] = hs[ts - 1:ts, :]
    o_ref[...] = (hs * jax.nn.gelu(yr)).astype(BF16)


def _rglru(x2d, g1, wxy, conv_w, conv_b, wa_bd, b_a, wx_bd, b_x, rg_lambda, batch, seq, ts):
    t, d = x2d.shape
    d_rnn = conv_w.shape[1]
    ns = seq // ts
    ng = d_rnn // MXU_TILE
    const2 = lambda b, s: (0, 0)
    const3 = lambda b, s: (0, 0, 0)
    return pl.pallas_call(
        functools.partial(_rglru_kernel, ts=ts, d_rnn=d_rnn),
        out_shape=jax.ShapeDtypeStruct((t, d_rnn), BF16),
        grid=(batch, ns),
        in_specs=[
            pl.BlockSpec((ts, d), lambda b, s: (b * ns + s, 0)),
            pl.BlockSpec((1, d), const2),
            pl.BlockSpec((d, 2 * d_rnn), const2),
            pl.BlockSpec((CONV_WIDTH, d_rnn), const2),
            pl.BlockSpec((1, d_rnn), const2),
            pl.BlockSpec((ng, MXU_TILE, MXU_TILE), const3),
            pl.BlockSpec((1, d_rnn), const2),
            pl.BlockSpec((ng, MXU_TILE, MXU_TILE), const3),
            pl.BlockSpec((1, d_rnn), const2),
            pl.BlockSpec((1, d_rnn), const2),
        ],
        out_specs=pl.BlockSpec((ts, d_rnn), lambda b, s: (b * ns + s, 0)),
        scratch_shapes=[pltpu.VMEM((ts + SUBLANES, d_rnn), F32), pltpu.VMEM((1, d_rnn), F32)],
        compiler_params=_params("parallel", "arbitrary"),
        name="rglru",
    )(x2d, g1, wxy, conv_w, conv_b, wa_bd, b_a, wx_bd, b_x, rg_lambda)


def _merge_kernel(x_ref, attn_ref, rnn_ref, g1_ref, wg_ref, bg_ref, pa_ref, pr_ref, wo_ref, g2_ref,
                  x2_ref, xn_ref, *, d):
    x = x_ref[...]
    h = _rms(x, g1_ref[...]).astype(BF16)
    gates = jax.nn.sigmoid(_dot(h, wg_ref[...]) + bg_ref[...])
    merged = gates[:, :d] * _dot(attn_ref[...], pa_ref[...]) + gates[:, d:] * _dot(rnn_ref[...], pr_ref[...])
    x2 = x + _dot(merged.astype(BF16), wo_ref[...])
    x2_ref[...] = x2
    xn_ref[...] = _rms(x2, g2_ref[...]).astype(BF16)


def _merge(x2d, attn, rnn, g1, wg, bg, pa, pr, wo, g2, tm):
    t, d = x2d.shape
    d_rnn = rnn.shape[1]
    const = lambda i: (0, 0)
    row = lambda i: (i, 0)
    return pl.pallas_call(
        functools.partial(_merge_kernel, d=d),
        out_shape=(jax.ShapeDtypeStruct((t, d), F32), jax.ShapeDtypeStruct((t, d), BF16)),
        grid=(t // tm,),
        in_specs=[
            pl.BlockSpec((tm, d), row), pl.BlockSpec((tm, d), row), pl.BlockSpec((tm, d_rnn), row),
            pl.BlockSpec((1, d), const), pl.BlockSpec((d, 2 * d), const), pl.BlockSpec((1, 2 * d), const),
            pl.BlockSpec((d, d), const), pl.BlockSpec((d_rnn, d), const), pl.BlockSpec((d, d), const),
            pl.BlockSpec((1, d), const),
        ],
        out_specs=(pl.BlockSpec((tm, d), row), pl.BlockSpec((tm, d), row)),
        compiler_params=_params("parallel"),
        name="merge",
    )(x2d, attn, rnn, g1, wg, bg, pa, pr, wo, g2)


def _top_values(s, k):
    vals = []
    cur = s
    for r in range(k):
        m = jnp.max(cur, axis=0, keepdims=True)
        vals.append(m)
        if r + 1 < k:
            cur = jnp.where(cur == m, -jnp.inf, cur)
    return vals


def _route_kernel(xn_ref, wqt_ref, keys_ref, thr_ref, e1_ref, s2_ref, e2_ref, *, tm):
    qt = _dot_nt(wqt_ref[...], xn_ref[...]).astype(BF16)
    nk = PEER_N_KEYS
    for h in range(PEER_HEADS):
        s1 = _dot(keys_ref[2 * h], qt[(2 * h) * nk:(2 * h + 1) * nk, :])
        s2 = _dot(keys_ref[2 * h + 1], qt[(2 * h + 1) * nk:(2 * h + 2) * nk, :])
        v1 = _top_values(s1, PEER_TOPK)
        v2 = _top_values(s2, PEER_TOPK)
        cands = [v1[i] + v2[j] for i in range(PEER_TOPK) for j in range(PEER_TOPK)
                 if (i + 1) * (j + 1) <= PEER_TOPK]
        pad = (-len(cands)) % SUBLANES
        cands += [jnp.full((1, tm), -jnp.inf, F32)] * pad
        best = _top_values(jnp.concatenate(cands, axis=0), PEER_TOPK)
        tau = best[PEER_TOPK - 1]
        z = sum(jnp.exp(b - best[0]) for b in best)
        thr_ref[h] = tau - s1
        e1_ref[h] = jnp.exp(s1 - v1[0]) / z
        s2_ref[h] = s2
        e2_ref[h] = jnp.exp(s2 - v2[0])


def _peer_route(xn, wqt, keys, tm):
    t, d = xn.shape
    nk = PEER_N_KEYS
    shp = jax.ShapeDtypeStruct((PEER_HEADS, nk, t), F32)
    ospec = pl.BlockSpec((PEER_HEADS, nk, tm), lambda i: (0, 0, i))
    return pl.pallas_call(
        functools.partial(_route_kernel, tm=tm),
        out_shape=(shp, shp, shp, shp),
        grid=(t // tm,),
        in_specs=[
            pl.BlockSpec((tm, d), lambda i: (i, 0)),
            pl.BlockSpec(wqt.shape, lambda i: (0, 0)),
            pl.BlockSpec(keys.shape, lambda i: (0, 0, 0)),
        ],
        out_specs=(ospec, ospec, ospec, ospec),
        compiler_params=_params("parallel"),
        name="peer_route",
    )(xn, wqt, keys)


def _peer_dense_kernel(xn_ref, x2_ref, u_ref, vt_ref, thr_ref, e1_ref, s2_ref, e2_ref, o_ref,
                       acc_ref, s_ref, w_ref, *, na):
    j = pl.program_id(1)

    @pl.when(j == 0)
    def _():
        acc_ref[...] = jnp.zeros(acc_ref.shape, F32)

    s_ref[...] = _dot_nt(u_ref[...], xn_ref[...])
    nk = PEER_N_KEYS

    def per_a(a, carry):
        rows = pl.ds(pl.multiple_of(a * nk, nk), nk)
        coef = jnp.zeros((nk, s_ref.shape[1]), F32)
        for h in range(PEER_HEADS):
            thr = thr_ref[h, pl.ds(a, 1), :]
            e1 = e1_ref[h, pl.ds(a, 1), :]
            coef = coef + e1 * jnp.where(s2_ref[h] >= thr, e2_ref[h], 0.0)
        w_ref[rows, :] = (coef * jax.nn.gelu(s_ref[rows, :])).astype(BF16)
        return carry

    lax.fori_loop(0, na, per_a, 0)
    acc_ref[...] += _dot(vt_ref[...], w_ref[...])

    @pl.when(j == pl.num_programs(1) - 1)
    def _():
        o_ref[...] = x2_ref[...] + acc_ref[...].T


def _peer_dense(xn, x2, u_bf, vt_bf, thr, e1, s2, e2, tm, na):
    t, d = xn.shape
    n_exp = u_bf.shape[0]
    nk = PEER_N_KEYS
    te = na * nk
    return pl.pallas_call(
        functools.partial(_peer_dense_kernel, na=na),
        out_shape=jax.ShapeDtypeStruct((t, d), F32),
        grid=(t // tm, n_exp // te),
        in_specs=[
            pl.BlockSpec((tm, d), lambda i, j: (i, 0)),
            pl.BlockSpec((tm, d), lambda i, j: (i, 0)),
            pl.BlockSpec((te, d), lambda i, j: (j, 0)),
            pl.BlockSpec((d, te), lambda i, j: (0, j)),
            pl.BlockSpec((PEER_HEADS, na, tm), lambda i, j: (0, j, i)),
            pl.BlockSpec((PEER_HEADS, na, tm), lambda i, j: (0, j, i)),
            pl.BlockSpec((PEER_HEADS, nk, tm), lambda i, j: (0, 0, i)),
            pl.BlockSpec((PEER_HEADS, nk, tm), lambda i, j: (0, 0, i)),
        ],
        out_specs=pl.BlockSpec((tm, d), lambda i, j: (i, 0)),
        scratch_shapes=[pltpu.VMEM((d, tm), F32), pltpu.VMEM((te, tm), F32), pltpu.VMEM((te, tm), BF16)],
        compiler_params=_params("parallel", "arbitrary"),
        name="peer_dense",
    )(xn, x2, u_bf, vt_bf, thr, e1, s2, e2)


def _block_diag_tiles(w):
    nb, r, _ = w.shape
    per = MXU_TILE // r
    w = w.reshape(nb // per, per, r, r)
    eye = jnp.eye(per, dtype=w.dtype)
    return jnp.einsum("gpij,pq->gpiqj", w, eye).reshape(nb // per, MXU_TILE, MXU_TILE)


def _layer(x, lam_init, norm1_g, w_in, b_gate, q_norm_g, k_norm_g, lambda_q1, lambda_k1, lambda_q2,
           lambda_k2, subln_g, conv_w, conv_b, w_rg_a, b_rg_a, w_rg_x, b_rg_x, rg_lambda, w_br_attn,
           w_br_rnn, w_out, norm2_g, w_peer_q, peer_sub_keys, peer_u, peer_v):
    batch, seq, d = x.shape
    t = batch * seq
    d_rnn = conv_w.shape[1]
    x2d = x.reshape(t, d)
    row = lambda v: v.reshape(1, -1)

    w_in_bf = w_in.astype(BF16)
    n_grp = d // HEAD_DIM
    scale = HEAD_DIM ** -0.5
    gn = jnp.stack([jnp.tile(q_norm_g, n_grp) * scale, jnp.tile(k_norm_g, n_grp),
                    jnp.ones((d,), F32)]).reshape(3, 1, d)
    gsum = jnp.kron(jnp.eye(n_grp, dtype=F32), jnp.full((HEAD_DIM, HEAD_DIM), 1.0 / HEAD_DIM, F32)).astype(BF16)
    wxy = w_in_bf[:, 3 * d:3 * d + 2 * d_rnn]
    wg = w_in_bf[:, 3 * d + 2 * d_rnn:]
    wa_bd = _block_diag_tiles(w_rg_a).astype(BF16)
    wx_bd = _block_diag_tiles(w_rg_x).astype(BF16)
    wqt = w_peer_q.T.astype(BF16)
    keys = peer_sub_keys.reshape(PEER_HEADS * 2, PEER_N_KEYS, -1).astype(BF16)
    u_bf = peer_u.astype(BF16)
    vt_bf = peer_v.T.astype(BF16)

    g1 = row(norm1_g)
    qkv = _qkv_proj(x2d, g1, w_in_bf, gn, gsum, tm=512)
    attn = _diff_attention(qkv, row(lambda_q1), row(lambda_k1), row(lambda_q2), row(lambda_k2),
                           row(subln_g), batch, seq, lam_init, tq=256)
    rnn = _rglru(x2d, g1, wxy, conv_w, row(conv_b), wa_bd, row(b_rg_a), wx_bd, row(b_rg_x),
                 row(rg_lambda), batch, seq, ts=256)
    x2, xn = _merge(x2d, attn, rnn, g1, wg, row(b_gate), w_br_attn.astype(BF16), w_br_rnn.astype(BF16),
                    w_out.astype(BF16), row(norm2_g), tm=256)
    thr, e1, s2, e2 = _peer_route(xn, wqt, keys, tm=256)
    out = _peer_dense(xn, x2, u_bf, vt_bf, thr, e1, s2, e2, tm=512, na=8)
    return out.reshape(batch, seq, d)


def kernel(x, norm1_g, w_in, b_gate, q_norm_g, k_norm_g, lambda_q1, lambda_k1, lambda_q2, lambda_k2,
           subln_g, conv_w, conv_b, w_rg_a, b_rg_a, w_rg_x, b_rg_x, rg_lambda, w_br_attn, w_br_rnn,
           w_out, norm2_g, w_peer_q, peer_sub_keys, peer_u, peer_v):
    depth = norm1_g.shape[0]
    for layer in range(depth):
        lam_init = 0.8 - 0.6 * math.exp(-0.3 * layer)
        x = _layer(x, lam_init, norm1_g[layer], w_in[layer], b_gate[layer], q_norm_g[layer],
                   k_norm_g[layer], lambda_q1[layer], lambda_k1[layer], lambda_q2[layer],
                   lambda_k2[layer], subln_g[layer], conv_w[layer], conv_b[layer], w_rg_a[layer],
                   b_rg_a[layer], w_rg_x[layer], b_rg_x[layer], rg_lambda[layer], w_br_attn[layer],
                   w_br_rnn[layer], w_out[layer], norm2_g[layer], w_peer_q[layer], peer_sub_keys[layer],
                   peer_u[layer], peer_v[layer])
    return x
```

```python
import functools
import math

import jax
import jax.numpy as jnp
from jax import lax
from jax.experimental import pallas as pl
from jax.experimental.pallas import tpu as pltpu

F32 = jnp.float32
BF16 = jnp.bfloat16

EPS = 1e-6
NEG_INF = -1e30
CHUNK = 64
N_HEADS = 8
HEAD_DIM = 64
V_HEAD_DIM = 128
RNN_BLOCK = 64
CONV_WIDTH = 4
RG_C = 8.0
PEER_HEADS = 8
PEER_N_KEYS = 128
PEER_TOPK = 16

MXU_TILE = 256
SUBLANES = 8
VMEM_LIMIT = 56 * 1024 * 1024


def _params(*sem):
    return pltpu.CompilerParams(dimension_semantics=sem, vmem_limit_bytes=VMEM_LIMIT)


def _rms(x, g):
    ms = jnp.mean(x * x, axis=-1, keepdims=True)
    return x * lax.rsqrt(ms + EPS) * g


def _dot(a, b):
    return jnp.dot(a, b, preferred_element_type=F32)


def _dot_nt(a, b):
    return lax.dot_general(a, b, (((1,), (1,)), ((), ())), preferred_element_type=F32)


def _qkv_kernel(x_ref, g1_ref, w_ref, gn_ref, gsum_ref, o_ref, h_ref):
    j = pl.program_id(1)

    @pl.when(j == 0)
    def _():
        h_ref[...] = _rms(x_ref[...], g1_ref[...]).astype(BF16)

    y = _dot(h_ref[...], w_ref[...])

    @pl.when(j < 2)
    def _():
        y2 = y * y
        hi = y2.astype(BF16)
        lo = (y2 - hi.astype(F32)).astype(BF16)
        ms = _dot(hi, gsum_ref[...]) + _dot(lo, gsum_ref[...])
        o_ref[...] = (y * lax.rsqrt(ms + EPS) * gn_ref[0]).astype(BF16)

    @pl.when(j == 2)
    def _():
        o_ref[...] = y.astype(BF16)


def _qkv_proj(x2d, g1, w_in_bf, gn, gsum, tm):
    t, d = x2d.shape
    return pl.pallas_call(
        _qkv_kernel,
        out_shape=jax.ShapeDtypeStruct((t, 3 * d), BF16),
        grid=(t // tm, 3),
        in_specs=[
            pl.BlockSpec((tm, d), lambda i, j: (i, 0)),
            pl.BlockSpec((1, d), lambda i, j: (0, 0)),
            pl.BlockSpec((d, d), lambda i, j: (0, j)),
            pl.BlockSpec((1, 1, d), lambda i, j: (j, 0, 0)),
            pl.BlockSpec((d, d), lambda i, j: (0, 0)),
        ],
        out_specs=pl.BlockSpec((tm, d), lambda i, j: (i, j)),
        scratch_shapes=[pltpu.VMEM((tm, d), BF16)],
        compiler_params=_params("parallel", "arbitrary"),
        name="qkv_proj",
    )(x2d, g1, w_in_bf, gn, gsum)


def _attn_kernel(q_ref, k_ref, v_ref, lq1_ref, lk1_ref, lq2_ref, lk2_ref, sg_ref, o_ref,
                 s_buf, mrun_ref, m_ref, acc_ref, vx_ref, *, tq, lam_init):
    i = pl.program_id(2)
    hd = V_HEAD_DIM

    @pl.when(i == 0)
    def _():
        vx_ref[:, :hd] = v_ref[...]
        vx_ref[:, hd:] = jnp.ones((vx_ref.shape[0], hd), BF16)

    q = q_ref[...]
    lane = lax.broadcasted_iota(jnp.int32, q.shape, 1)
    zero = jnp.zeros_like(q)
    qs = jnp.concatenate([jnp.where(lane < HEAD_DIM, q, zero), jnp.where(lane >= HEAD_DIM, q, zero)], axis=0)

    mrun_ref[...] = jnp.full(mrun_ref.shape, NEG_INF, F32)

    def scores(j, masked):
        start = pl.multiple_of(j * tq, tq)
        s = _dot_nt(qs, k_ref[pl.ds(start, tq), :])
        if masked:
            r = lax.broadcasted_iota(jnp.int32, s.shape, 0)
            c = lax.broadcasted_iota(jnp.int32, s.shape, 1)
            qpos = jnp.where(r >= tq, r - tq, r)
            s = jnp.where((c // CHUNK) <= (qpos // CHUNK), s, NEG_INF)
        s_buf[j] = s
        m = mrun_ref[...]
        for c0 in range(0, tq, hd):
            m = jnp.maximum(m, s[:, c0:c0 + hd])
        mrun_ref[...] = m

    def full_scores(j, carry):
        scores(j, False)
        return carry

    lax.fori_loop(0, i, full_scores, 0)
    scores(i, True)
    m_ref[...] = jnp.broadcast_to(jnp.max(mrun_ref[...], axis=-1, keepdims=True), m_ref.shape)

    acc_ref[...] = jnp.zeros(acc_ref.shape, F32)

    def weighted(j, carry):
        start = pl.multiple_of(j * tq, tq)
        s = s_buf[j]
        m = m_ref[...]
        p = jnp.concatenate([jnp.exp(s[:, c0:c0 + hd] - m) for c0 in range(0, tq, hd)], axis=-1)
        acc_ref[...] += _dot(p.astype(BF16), vx_ref[pl.ds(start, tq), :])
        return carry

    lax.fori_loop(0, i + 1, weighted, 0)

    lam = (jnp.exp(jnp.sum(lq1_ref[...] * lk1_ref[...], axis=-1, keepdims=True))
           - jnp.exp(jnp.sum(lq2_ref[...] * lk2_ref[...], axis=-1, keepdims=True)) + lam_init)
    o = acc_ref[:, :hd] / acc_ref[:, hd:]
    o = o[:tq] - lam * o[tq:]
    o_ref[...] = (_rms(o, sg_ref[...]) * (1.0 - lam_init)).astype(BF16)


def _diff_attention(qkv, lq1, lk1, lq2, lk2, subln_g, batch, seq, lam_init, tq):
    t = qkv.shape[0]
    nq = seq // tq
    hd = V_HEAD_DIM
    small = pl.BlockSpec((1, HEAD_DIM), lambda b, h, i: (0, 0))
    return pl.pallas_call(
        functools.partial(_attn_kernel, tq=tq, lam_init=lam_init),
        out_shape=jax.ShapeDtypeStruct((t, N_HEADS * hd), BF16),
        grid=(batch, N_HEADS, nq),
        in_specs=[
            pl.BlockSpec((tq, hd), lambda b, h, i: (b * nq + i, h)),
            pl.BlockSpec((seq, hd), lambda b, h, i: (b, N_HEADS + h)),
            pl.BlockSpec((seq, hd), lambda b, h, i: (b, 2 * N_HEADS + h)),
            small, small, small, small,
            pl.BlockSpec((1, hd), lambda b, h, i: (0, 0)),
        ],
        out_specs=pl.BlockSpec((tq, hd), lambda b, h, i: (b * nq + i, h)),
        scratch_shapes=[pltpu.VMEM((nq, 2 * tq, tq), F32), pltpu.VMEM((2 * tq, hd), F32),
                        pltpu.VMEM((2 * tq, hd), F32), pltpu.VMEM((2 * tq, 2 * hd), F32),
                        pltpu.VMEM((seq, 2 * hd), BF16)],
        compiler_params=_params("arbitrary", "arbitrary", "arbitrary"),
        name="diff_attn",
    )(qkv, qkv, qkv, lq1, lk1, lq2, lk2, subln_g)


def _rglru_kernel(x_ref, g1_ref, wxy_ref, cw_ref, cb_ref, wa_ref, ba_ref, wx_ref, bx_ref, lam_ref, o_ref,
                  xbuf_ref, h_ref, *, ts, d_rnn):
    s_idx = pl.program_id(1)

    @pl.when(s_idx == 0)
    def _():
        xbuf_ref[0:SUBLANES, :] = jnp.zeros((SUBLANES, d_rnn), F32)
        h_ref[...] = jnp.zeros(h_ref.shape, F32)

    h_in = _rms(x_ref[...], g1_ref[...]).astype(BF16)
    xy = _dot(h_in, wxy_ref[...])
    xr = xy[:, :d_rnn]
    yr = xy[:, d_rnn:]

    xbuf_ref[SUBLANES:SUBLANES + ts, :] = xr
    conv = cb_ref[...] + jnp.zeros((ts, d_rnn), F32)
    for tap in range(CONV_WIDTH):
        off = SUBLANES - (CONV_WIDTH - 1) + tap
        conv = conv + xbuf_ref[off:off + ts, :] * cw_ref[tap:tap + 1, :]
    xbuf_ref[0:SUBLANES, :] = xbuf_ref[ts:ts + SUBLANES, :]

    cbf = conv.astype(BF16)
    ga, gi = [], []
    for g in range(d_rnn // MXU_TILE):
        sl = slice(g * MXU_TILE, (g + 1) * MXU_TILE)
        ga.append(_dot(cbf[:, sl], wa_ref[g]))
        gi.append(_dot(cbf[:, sl], wx_ref[g]))
    gate_r = jax.nn.sigmoid(jnp.concatenate(ga, axis=-1) + ba_ref[...])
    gate_i = jax.nn.sigmoid(jnp.concatenate(gi, axis=-1) + bx_ref[...])

    neg_lam = -lam_ref[...]
    softplus = jnp.maximum(neg_lam, 0.0) + jnp.log1p(jnp.exp(-jnp.abs(neg_lam)))
    log_a = -RG_C * gate_r * softplus
    a = jnp.exp(log_a)
    th = jnp.tanh(log_a)
    mult = jnp.sqrt(-2.0 * th / (1.0 - th))
    u = mult * (gate_i * conv)

    row = lax.broadcasted_iota(jnp.int32, (ts, d_rnn), 0)
    d = 1
    while d < ts:
        keep = row >= d
        a_sh = jnp.where(keep, pltpu.roll(a, d, 0), 1.0)
        u_sh = jnp.where(keep, pltpu.roll(u, d, 0), 0.0)
        u = a * u_sh + u
        a = a * a_sh
        d *= 2
    hs = u + a * h_ref[...]
    h_ref[...] = hs[ts - 1:ts, :]
    o_ref[...] = (hs * jax.nn.gelu(yr)).astype(BF16)


def _rglru(x2d, g1, wxy, conv_w, conv_b, wa_bd, b_a, wx_bd, b_x, rg_lambda, batch, seq, ts):
    t, d = x2d.shape
    d_rnn = conv_w.shape[1]
    ns = seq // ts
    ng = d_rnn // MXU_TILE
    const2 = lambda b, s: (0, 0)
    const3 = lambda b, s: (0, 0, 0)
    return pl.pallas_call(
        functools.partial(_rglru_kernel, ts=ts, d_rnn=d_rnn),
        out_shape=jax.ShapeDtypeStruct((t, d_rnn), BF16),
        grid=(batch, ns),
        in_specs=[
            pl.BlockSpec((ts, d), lambda b, s: (b * ns + s, 0)),
            pl.BlockSpec((1, d), const2),
            pl.BlockSpec((d, 2 * d_rnn), const2),
            pl.BlockSpec((CONV_WIDTH, d_rnn), const2),
            pl.BlockSpec((1, d_rnn), const2),
            pl.BlockSpec((ng, MXU_TILE, MXU_TILE), const3),
            pl.BlockSpec((1, d_rnn), const2),
            pl.BlockSpec((ng, MXU_TILE, MXU_TILE), const3),
            pl.BlockSpec((1, d_rnn), const2),
            pl.BlockSpec((1, d_rnn), const2),
        ],
        out_specs=pl.BlockSpec((ts, d_rnn), lambda b, s: (b * ns + s, 0)),
        scratch_shapes=[pltpu.VMEM((ts + SUBLANES, d_rnn), F32), pltpu.VMEM((1, d_rnn), F32)],
        compiler_params=_params("parallel", "arbitrary"),
        name="rglru",
    )(x2d, g1, wxy, conv_w, conv_b, wa_bd, b_a, wx_bd, b_x, rg_lambda)


def _merge_kernel(x_ref, attn_ref, rnn_ref, g1_ref, wg_ref, bg_ref, pa_ref, pr_ref, wo_ref, g2_ref,
                  x2_ref, xn_ref, *, d):
    x = x_ref[...]
    h = _rms(x, g1_ref[...]).astype(BF16)
    gates = jax.nn.sigmoid(_dot(h, wg_ref[...]) + bg_ref[...])
    merged = gates[:, :d] * _dot(attn_ref[...], pa_ref[...]) + gates[:, d:] * _dot(rnn_ref[...], pr_ref[...])
    x2 = x + _dot(merged.astype(BF16), wo_ref[...])
    x2_ref[...] = x2
    xn_ref[...] = _rms(x2, g2_ref[...]).astype(BF16)


def _merge(x2d, attn, rnn, g1, wg, bg, pa, pr, wo, g2, tm):
    t, d = x2d.shape
    d_rnn = rnn.shape[1]
    const = lambda i: (0, 0)
    row = lambda i: (i, 0)
    return pl.pallas_call(
        functools.partial(_merge_kernel, d=d),
        out_shape=(jax.ShapeDtypeStruct((t, d), F32), jax.ShapeDtypeStruct((t, d), BF16)),
        grid=(t // tm,),
        in_specs=[
            pl.BlockSpec((tm, d), row), pl.BlockSpec((tm, d), row), pl.BlockSpec((tm, d_rnn), row),
            pl.BlockSpec((1, d), const), pl.BlockSpec((d, 2 * d), const), pl.BlockSpec((1, 2 * d), const),
            pl.BlockSpec((d, d), const), pl.BlockSpec((d_rnn, d), const), pl.BlockSpec((d, d), const),
            pl.BlockSpec((1, d), const),
        ],
        out_specs=(pl.BlockSpec((tm, d), row), pl.BlockSpec((tm, d), row)),
        compiler_params=_params("parallel"),
        name="merge",
    )(x2d, attn, rnn, g1, wg, bg, pa, pr, wo, g2)


def _top_values(s, k):
    vals = []
    cur = s
    for r in range(k):
        m = jnp.max(cur, axis=0, keepdims=True)
        vals.append(m)
        if r + 1 < k:
            cur = jnp.where(cur == m, -jnp.inf, cur)
    return vals


def _route_kernel(xn_ref, wqt_ref, keys_ref, thr_ref, e1_ref, s2_ref, e2_ref, *, tm):
    qt = _dot_nt(wqt_ref[...], xn_ref[...]).astype(BF16)
    nk = PEER_N_KEYS
    for h in range(PEER_HEADS):
        s1 = _dot(keys_ref[2 * h], qt[(2 * h) * nk:(2 * h + 1) * nk, :])
        s2 = _dot(keys_ref[2 * h + 1], qt[(2 * h + 1) * nk:(2 * h + 2) * nk, :])
        v1 = _top_values(s1, PEER_TOPK)
        v2 = _top_values(s2, PEER_TOPK)
        cands = [v1[i] + v2[j] for i in range(PEER_TOPK) for j in range(PEER_TOPK)
                 if (i + 1) * (j + 1) <= PEER_TOPK]
        pad = (-len(cands)) % SUBLANES
        cands += [jnp.full((1, tm), -jnp.inf, F32)] * pad
        best = _top_values(jnp.concatenate(cands, axis=0), PEER_TOPK)
        tau = best[PEER_TOPK - 1]
        z = sum(jnp.exp(b - best[0]) for b in best)
        thr_ref[h] = tau - s1
        e1_ref[h] = jnp.exp(s1 - v1[0]) / z
        s2_ref[h] = s2
        e2_ref[h] = jnp.exp(s2 - v2[0])


def _peer_route(xn, wqt, keys, tm):
    t, d = xn.shape
    nk = PEER_N_KEYS
    shp = jax.ShapeDtypeStruct((PEER_HEADS, nk, t), F32)
    ospec = pl.BlockSpec((PEER_HEADS, nk, tm), lambda i: (0, 0, i))
    return pl.pallas_call(
        functools.partial(_route_kernel, tm=tm),
        out_shape=(shp, shp, shp, shp),
        grid=(t // tm,),
        in_specs=[
            pl.BlockSpec((tm, d), lambda i: (i, 0)),
            pl.BlockSpec(wqt.shape, lambda i: (0, 0)),
            pl.BlockSpec(keys.shape, lambda i: (0, 0, 0)),
        ],
        out_specs=(ospec, ospec, ospec, ospec),
        compiler_params=_params("parallel"),
        name="peer_route",
    )(xn, wqt, keys)


def _peer_dense_kernel(xn_ref, x2_ref, u_ref, vt_ref, thr_ref, e1_ref, s2_ref, e2_ref, o_ref,
                       acc_ref, s_ref, w_ref, *, na):
    j = pl.program_id(1)

    @pl.when(j == 0)
    def _():
        acc_ref[...] = jnp.zeros(acc_ref.shape, F32)

    s_ref[...] = _dot_nt(u_ref[...], xn_ref[...])
    nk = PEER_N_KEYS

    def per_a(a, carry):
        rows = pl.ds(pl.multiple_of(a * nk, nk), nk)
        coef = jnp.zeros((nk, s_ref.shape[1]), F32)
        for h in range(PEER_HEADS):
            thr = thr_ref[h, pl.ds(a, 1), :]
            e1 = e1_ref[h, pl.ds(a, 1), :]
            coef = coef + e1 * jnp.where(s2_ref[h] >= thr, e2_ref[h], 0.0)
        w_ref[rows, :] = (coef * jax.nn.gelu(s_ref[rows, :])).astype(BF16)
        return carry

    lax.fori_loop(0, na, per_a, 0)
    acc_ref[...] += _dot(vt_ref[...], w_ref[...])

    @pl.when(j == pl.num_programs(1) - 1)
    def _():
        o_ref[...] = x2_ref[...] + acc_ref[...].T


def _peer_dense(xn, x2, u_bf, vt_bf, thr, e1, s2, e2, tm, na):
    t, d = xn.shape
    n_exp = u_bf.shape[0]
    nk = PEER_N_KEYS
    te = na * nk
    return pl.pallas_call(
        functools.partial(_peer_dense_kernel, na=na),
        out_shape=jax.ShapeDtypeStruct((t, d), F32),
        grid=(t // tm, n_exp // te),
        in_specs=[
            pl.BlockSpec((tm, d), lambda i, j: (i, 0)),
            pl.BlockSpec((tm, d), lambda i, j: (i, 0)),
            pl.BlockSpec((te, d), lambda i, j: (j, 0)),
            pl.BlockSpec((d, te), lambda i, j: (0, j)),
            pl.BlockSpec((PEER_HEADS, na, tm), lambda i, j: (0, j, i)),
            pl.BlockSpec((PEER_HEADS, na, tm), lambda i, j: (0, j, i)),
            pl.BlockSpec((PEER_HEADS, nk, tm), lambda i, j: (0, 0, i)),
            pl.BlockSpec((PEER_HEADS, nk, tm), lambda i, j: (0, 0, i)),
        ],
        out_specs=pl.BlockSpec((tm, d), lambda i, j: (i, 0)),
        scratch_shapes=[pltpu.VMEM((d, tm), F32), pltpu.VMEM((te, tm), F32), pltpu.VMEM((te, tm), BF16)],
        compiler_params=_params("parallel", "arbitrary"),
        name="peer_dense",
    )(xn, x2, u_bf, vt_bf, thr, e1, s2, e2)


def _block_diag_tiles(w):
    nb, r, _ = w.shape
    per = MXU_TILE // r
    w = w.reshape(nb // per, per, r, r)
    eye = jnp.eye(per, dtype=w.dtype)
    return jnp.einsum("gpij,pq->gpiqj", w, eye).reshape(nb // per, MXU_TILE, MXU_TILE)


def _layer(x, lam_init, norm1_g, w_in, b_gate, q_norm_g, k_norm_g, lambda_q1, lambda_k1, lambda_q2,
           lambda_k2, subln_g, conv_w, conv_b, w_rg_a, b_rg_a, w_rg_x, b_rg_x, rg_lambda, w_br_attn,
           w_br_rnn, w_out, norm2_g, w_peer_q, peer_sub_keys, peer_u, peer_v):
    batch, seq, d = x.shape
    t = batch * seq
    d_rnn = conv_w.shape[1]
    x2d = x.reshape(t, d)
    row = lambda v: v.reshape(1, -1)

    w_in_bf = w_in.astype(BF16)
    n_grp = d // HEAD_DIM
    scale = HEAD_DIM ** -0.5
    gn = jnp.stack([jnp.tile(q_norm_g, n_grp) * scale, jnp.tile(k_norm_g, n_grp),
                    jnp.ones((d,), F32)]).reshape(3, 1, d)
    gsum = jnp.kron(jnp.eye(n_grp, dtype=F32), jnp.full((HEAD_DIM, HEAD_DIM), 1.0 / HEAD_DIM, F32)).astype(BF16)
    wxy = w_in_bf[:, 3 * d:3 * d + 2 * d_rnn]
    wg = w_in_bf[:, 3 * d + 2 * d_rnn:]
    wa_bd = _block_diag_tiles(w_rg_a).astype(BF16)
    wx_bd = _block_diag_tiles(w_rg_x).astype(BF16)
    wqt = w_peer_q.T.astype(BF16)
    keys = peer_sub_keys.reshape(PEER_HEADS * 2, PEER_N_KEYS, -1).astype(BF16)
    u_bf = peer_u.astype(BF16)
    vt_bf = peer_v.T.astype(BF16)

    g1 = row(norm1_g)
    qkv = _qkv_proj(x2d, g1, w_in_bf, gn, gsum, tm=512)
    attn = _diff_attention(qkv, row(lambda_q1), row(lambda_k1), row(lambda_q2), row(lambda_k2),
                           row(subln_g), batch, seq, lam_init, tq=512)
    rnn = _rglru(x2d, g1, wxy, conv_w, row(conv_b), wa_bd, row(b_rg_a), wx_bd, row(b_rg_x),
                 row(rg_lambda), batch, seq, ts=256)
    x2, xn = _merge(x2d, attn, rnn, g1, wg, row(b_gate), w_br_attn.astype(BF16), w_br_rnn.astype(BF16),
                    w_out.astype(BF16), row(norm2_g), tm=256)
    thr, e1, s2, e2 = _peer_route(xn, wqt, keys, tm=256)
    out = _peer_dense(xn, x2, u_bf, vt_bf, thr, e1, s2, e2, tm=512, na=8)
    return out.reshape(batch, seq, d)


def kernel(x, norm1_g, w_in, b_gate, q_norm_g, k_norm_g, lambda_q1, lambda_k1, lambda_q2, lambda_k2,
           subln_g, conv_w, conv_b, w_rg_a, b_rg_a, w_rg_x, b_rg_x, rg_lambda, w_br_attn, w_br_rnn,
           w_out, norm2_g, w_peer_q, peer_sub_keys, peer_u, peer_v):
    depth = norm1_g.shape[0]
    for layer in range(depth):
        lam_init = 0.8 - 0.6 * math.exp(-0.3 * layer)
        x = _layer(x, lam_init, norm1_g[layer], w_in[layer], b_gate[layer], q_norm_g[layer],
                   k_norm_g[layer], lambda_q1[layer], lambda_k1[layer], lambda_q2[layer],
                   lambda_k2[layer], subln_g[layer], conv_w[layer], conv_b[layer], w_rg_a[layer],
                   b_rg_a[layer], w_rg_x[layer], b_rg_x[layer], rg_lambda[layer], w_br_attn[layer],
                   w_br_rnn[layer], w_out[layer], norm2_g[layer], w_peer_q[layer], peer_sub_keys[layer],
                   peer_u[layer], peer_v[layer])
    return x
```

```python
import functools
import math

import jax
import jax.numpy as jnp
from jax import lax
from jax.experimental import pallas as pl
from jax.experimental.pallas import tpu as pltpu

F32 = jnp.float32
BF16 = jnp.bfloat16

EPS = 1e-6
NEG_INF = -1e30
CHUNK = 64
N_HEADS = 8
HEAD_DIM = 64
V_HEAD_DIM = 128
RNN_BLOCK = 64
CONV_WIDTH = 4
RG_C = 8.0
PEER_HEADS = 8
PEER_N_KEYS = 128
PEER_TOPK = 16

MXU_TILE = 256
SUBLANES = 8
VMEM_LIMIT = 56 * 1024 * 1024


def _params(*sem):
    return pltpu.CompilerParams(dimension_semantics=sem, vmem_limit_bytes=VMEM_LIMIT)


def _rms(x, g):
    ms = jnp.mean(x * x, axis=-1, keepdims=True)
    return x * lax.rsqrt(ms + EPS) * g


def _dot(a, b):
    return jnp.dot(a, b, preferred_element_type=F32)


def _dot_nt(a, b):
    return lax.dot_general(a, b, (((1,), (1,)), ((), ())), preferred_element_type=F32)


def _qkv_kernel(x_ref, g1_ref, w_ref, gn_ref, gsum_ref, o_ref, h_ref):
    j = pl.program_id(1)

    @pl.when(j == 0)
    def _():
        h_ref[...] = _rms(x_ref[...], g1_ref[...]).astype(BF16)

    y = _dot(h_ref[...], w_ref[...])

    @pl.when(j < 2)
    def _():
        y2 = y * y
        hi = y2.astype(BF16)
        lo = (y2 - hi.astype(F32)).astype(BF16)
        g = gsum_ref[...]
        ms = jnp.concatenate(
            [_dot(hi[:, c:c + MXU_TILE], g) + _dot(lo[:, c:c + MXU_TILE], g)
             for c in range(0, y.shape[1], MXU_TILE)], axis=-1)
        o_ref[...] = (y * lax.rsqrt(ms + EPS) * gn_ref[0]).astype(BF16)

    @pl.when(j == 2)
    def _():
        o_ref[...] = y.astype(BF16)


def _qkv_proj(x2d, g1, w_in_bf, gn, gsum, tm):
    t, d = x2d.shape
    return pl.pallas_call(
        _qkv_kernel,
        out_shape=jax.ShapeDtypeStruct((t, 3 * d), BF16),
        grid=(t // tm, 3),
        in_specs=[
            pl.BlockSpec((tm, d), lambda i, j: (i, 0)),
            pl.BlockSpec((1, d), lambda i, j: (0, 0)),
            pl.BlockSpec((d, d), lambda i, j: (0, j)),
            pl.BlockSpec((1, 1, d), lambda i, j: (j, 0, 0)),
            pl.BlockSpec((MXU_TILE, MXU_TILE), lambda i, j: (0, 0)),
        ],
        out_specs=pl.BlockSpec((tm, d), lambda i, j: (i, j)),
        scratch_shapes=[pltpu.VMEM((tm, d), BF16)],
        compiler_params=_params("parallel", "arbitrary"),
        name="qkv_proj",
    )(x2d, g1, w_in_bf, gn, gsum)


def _attn_kernel(q_ref, k_ref, v_ref, lq1_ref, lk1_ref, lq2_ref, lk2_ref, sg_ref, o_ref,
                 s_buf, mrun_ref, m_ref, acc_ref, vx_ref, *, tq, lam_init):
    i = pl.program_id(2)
    hd = V_HEAD_DIM

    @pl.when(i == 0)
    def _():
        vx_ref[:, :hd] = v_ref[...]
        vx_ref[:, hd:] = jnp.ones((vx_ref.shape[0], hd), BF16)

    q = q_ref[...]
    lane = lax.broadcasted_iota(jnp.int32, q.shape, 1)
    zero = jnp.zeros_like(q)
    qs = jnp.concatenate([jnp.where(lane < HEAD_DIM, q, zero), jnp.where(lane >= HEAD_DIM, q, zero)], axis=0)

    mrun_ref[...] = jnp.full(mrun_ref.shape, NEG_INF, F32)

    def scores(j, masked):
        start = pl.multiple_of(j * tq, tq)
        s = _dot_nt(qs, k_ref[pl.ds(start, tq), :])
        if masked:
            r = lax.broadcasted_iota(jnp.int32, s.shape, 0)
            c = lax.broadcasted_iota(jnp.int32, s.shape, 1)
            qpos = jnp.where(r >= tq, r - tq, r)
            s = jnp.where((c // CHUNK) <= (qpos // CHUNK), s, NEG_INF)
        s_buf[j] = s
        m = mrun_ref[...]
        for c0 in range(0, tq, hd):
            m = jnp.maximum(m, s[:, c0:c0 + hd])
        mrun_ref[...] = m

    def full_scores(j, carry):
        scores(j, False)
        return carry

    lax.fori_loop(0, i, full_scores, 0)
    scores(i, True)
    m_ref[...] = jnp.broadcast_to(jnp.max(mrun_ref[...], axis=-1, keepdims=True), m_ref.shape)

    acc_ref[...] = jnp.zeros(acc_ref.shape, F32)

    def weighted(j, carry):
        start = pl.multiple_of(j * tq, tq)
        s = s_buf[j]
        m = m_ref[...]
        p = jnp.concatenate([jnp.exp(s[:, c0:c0 + hd] - m) for c0 in range(0, tq, hd)], axis=-1)
        acc_ref[...] += _dot(p.astype(BF16), vx_ref[pl.ds(start, tq), :])
        return carry

    lax.fori_loop(0, i + 1, weighted, 0)

    lam = (jnp.exp(jnp.sum(lq1_ref[...] * lk1_ref[...], axis=-1, keepdims=True))
           - jnp.exp(jnp.sum(lq2_ref[...] * lk2_ref[...], axis=-1, keepdims=True)) + lam_init)
    o = acc_ref[:, :hd] / acc_ref[:, hd:]
    o = o[:tq] - lam * o[tq:]
    o_ref[...] = (_rms(o, sg_ref[...]) * (1.0 - lam_init)).astype(BF16)


def _diff_attention(qkv, lq1, lk1, lq2, lk2, subln_g, batch, seq, lam_init, tq):
    t = qkv.shape[0]
    nq = seq // tq
    hd = V_HEAD_DIM
    small = pl.BlockSpec((1, HEAD_DIM), lambda b, h, i: (0, 0))
    return pl.pallas_call(
        functools.partial(_attn_kernel, tq=tq, lam_init=lam_init),
        out_shape=jax.ShapeDtypeStruct((t, N_HEADS * hd), BF16),
        grid=(batch, N_HEADS, nq),
        in_specs=[
            pl.BlockSpec((tq, hd), lambda b, h, i: (b * nq + i, h)),
            pl.BlockSpec((seq, hd), lambda b, h, i: (b, N_HEADS + h)),
            pl.BlockSpec((seq, hd), lambda b, h, i: (b, 2 * N_HEADS + h)),
            small, small, small, small,
            pl.BlockSpec((1, hd), lambda b, h, i: (0, 0)),
        ],
        out_specs=pl.BlockSpec((tq, hd), lambda b, h, i: (b * nq + i, h)),
        scratch_shapes=[pltpu.VMEM((nq, 2 * tq, tq), F32), pltpu.VMEM((2 * tq, hd), F32),
                        pltpu.VMEM((2 * tq, hd), F32), pltpu.VMEM((2 * tq, 2 * hd), F32),
                        pltpu.VMEM((seq, 2 * hd), BF16)],
        compiler_params=_params("arbitrary", "arbitrary", "arbitrary"),
        name="diff_attn",
    )(qkv, qkv, qkv, lq1, lk1, lq2, lk2, subln_g)


def _rglru_kernel(x_ref, g1_ref, wxy_ref, cw_ref, cb_ref, wa_ref, ba_ref, wx_ref, bx_ref, lam_ref, o_ref,
                  xbuf_ref, h_ref, *, ts, d_rnn):
    s_idx = pl.program_id(1)

    @pl.when(s_idx == 0)
    def _():
        xbuf_ref[0:SUBLANES, :] = jnp.zeros((SUBLANES, d_rnn), F32)
        h_ref[...] = jnp.zeros(h_ref.shape, F32)

    h_in = _rms(x_ref[...], g1_ref[...]).astype(BF16)
    xy = _dot(h_in, wxy_ref[...])
    xr = xy[:, :d_rnn]
    yr = xy[:, d_rnn:]

    xbuf_ref[SUBLANES:SUBLANES + ts, :] = xr
    conv = cb_ref[...] + jnp.zeros((ts, d_rnn), F32)
    for tap in range(CONV_WIDTH):
        off = SUBLANES - (CONV_WIDTH - 1) + tap
        conv = conv + xbuf_ref[off:off + ts, :] * cw_ref[tap:tap + 1, :]
    xbuf_ref[0:SUBLANES, :] = xbuf_ref[ts:ts + SUBLANES, :]

    cbf = conv.astype(BF16)
    ga, gi = [], []
    for g in range(d_rnn // MXU_TILE):
        sl = slice(g * MXU_TILE, (g + 1) * MXU_TILE)
        ga.append(_dot(cbf[:, sl], wa_ref[g]))
        gi.append(_dot(cbf[:, sl], wx_ref[g]))
    gate_r = jax.nn.sigmoid(jnp.concatenate(ga, axis=-1) + ba_ref[...])
    gate_i = jax.nn.sigmoid(jnp.concatenate(gi, axis=-1) + bx_ref[...])

    neg_lam = -lam_ref[...]
    softplus = jnp.maximum(neg_lam, 0.0) + jnp.log1p(jnp.exp(-jnp.abs(neg_lam)))
    log_a = -RG_C * gate_r * softplus
    a = jnp.exp(log_a)
    th = jnp.tanh(log_a)
    mult = jnp.sqrt(-2.0 * th / (1.0 - th))
    u = mult * (gate_i * conv)

    row = lax.broadcasted_iota(jnp.int32, (ts, d_rnn), 0)
    d = 1
    while d < ts:
        keep = row >= d
        a_sh = jnp.where(keep, pltpu.roll(a, d, 0), 1.0)
        u_sh = jnp.where(keep, pltpu.roll(u, d, 0), 0.0)
        u = a * u_sh + u
        a = a * a_sh
        d *= 2
    hs = u + a * h_ref[...]
    h_ref[...] = hs[ts - 1:ts, :]
    o_ref[...] = (hs * jax.nn.gelu(yr)).astype(BF16)


def _rglru(x2d, g1, wxy, conv_w, conv_b, wa_bd, b_a, wx_bd, b_x, rg_lambda, batch, seq, ts):
    t, d = x2d.shape
    d_rnn = conv_w.shape[1]
    ns = seq // ts
    ng = d_rnn // MXU_TILE
    const2 = lambda b, s: (0, 0)
    const3 = lambda b, s: (0, 0, 0)
    return pl.pallas_call(
        functools.partial(_rglru_kernel, ts=ts, d_rnn=d_rnn),
        out_shape=jax.ShapeDtypeStruct((t, d_rnn), BF16),
        grid=(batch, ns),
        in_specs=[
            pl.BlockSpec((ts, d), lambda b, s: (b * ns + s, 0)),
            pl.BlockSpec((1, d), const2),
            pl.BlockSpec((d, 2 * d_rnn), const2),
            pl.BlockSpec((CONV_WIDTH, d_rnn), const2),
            pl.BlockSpec((1, d_rnn), const2),
            pl.BlockSpec((ng, MXU_TILE, MXU_TILE), const3),
            pl.BlockSpec((1, d_rnn), const2),
            pl.BlockSpec((ng, MXU_TILE, MXU_TILE), const3),
            pl.BlockSpec((1, d_rnn), const2),
            pl.BlockSpec((1, d_rnn), const2),
        ],
        out_specs=pl.BlockSpec((ts, d_rnn), lambda b, s: (b * ns + s, 0)),
        scratch_shapes=[pltpu.VMEM((ts + SUBLANES, d_rnn), F32), pltpu.VMEM((1, d_rnn), F32)],
        compiler_params=_params("parallel", "arbitrary"),
        name="rglru",
    )(x2d, g1, wxy, conv_w, conv_b, wa_bd, b_a, wx_bd, b_x, rg_lambda)


def _merge_kernel(x_ref, attn_ref, rnn_ref, g1_ref, wg_ref, bg_ref, pa_ref, pr_ref, wo_ref, g2_ref,
                  x2_ref, xn_ref, *, d):
    x = x_ref[...]
    h = _rms(x, g1_ref[...]).astype(BF16)
    gates = jax.nn.sigmoid(_dot(h, wg_ref[...]) + bg_ref[...])
    merged = gates[:, :d] * _dot(attn_ref[...], pa_ref[...]) + gates[:, d:] * _dot(rnn_ref[...], pr_ref[...])
    x2 = x + _dot(merged.astype(BF16), wo_ref[...])
    x2_ref[...] = x2
    xn_ref[...] = _rms(x2, g2_ref[...]).astype(BF16)


def _merge(x2d, attn, rnn, g1, wg, bg, pa, pr, wo, g2, tm):
    t, d = x2d.shape
    d_rnn = rnn.shape[1]
    const = lambda i: (0, 0)
    row = lambda i: (i, 0)
    return pl.pallas_call(
        functools.partial(_merge_kernel, d=d),
        out_shape=(jax.ShapeDtypeStruct((t, d), F32), jax.ShapeDtypeStruct((t, d), BF16)),
        grid=(t // tm,),
        in_specs=[
            pl.BlockSpec((tm, d), row), pl.BlockSpec((tm, d), row), pl.BlockSpec((tm, d_rnn), row),
            pl.BlockSpec((1, d), const), pl.BlockSpec((d, 2 * d), const), pl.BlockSpec((1, 2 * d), const),
            pl.BlockSpec((d, d), const), pl.BlockSpec((d_rnn, d), const), pl.BlockSpec((d, d), const),
            pl.BlockSpec((1, d), const),
        ],
        out_specs=(pl.BlockSpec((tm, d), row), pl.BlockSpec((tm, d), row)),
        compiler_params=_params("parallel"),
        name="merge",
    )(x2d, attn, rnn, g1, wg, bg, pa, pr, wo, g2)


def _top_values(s, k, with_rank=False):
    vals = []
    cur = s
    rank = jnp.full(s.shape, float(k), F32) if with_rank else None
    for r in range(k):
        m = jnp.max(cur, axis=0, keepdims=True)
        vals.append(m)
        eq = cur == m
        if with_rank:
            rank = jnp.where(eq, float(r), rank)
        if r + 1 < k:
            cur = jnp.where(eq, -jnp.inf, cur)
    return (vals, rank) if with_rank else vals


def _twice_bf16(x):
    bits = lax.bitcast_convert_type(x.astype(BF16).astype(F32), jnp.uint32)
    return bits | (bits >> 16)


def _route_kernel(xn_ref, wqt_ref, keys_ref, n_ref, e1_ref, rank2_ref, e2_ref, *, tm):
    qt = _dot_nt(wqt_ref[...], xn_ref[...]).astype(BF16)
    nk = PEER_N_KEYS
    for h in range(PEER_HEADS):
        s1 = _dot(keys_ref[2 * h], qt[(2 * h) * nk:(2 * h + 1) * nk, :])
        s2 = _dot(keys_ref[2 * h + 1], qt[(2 * h + 1) * nk:(2 * h + 2) * nk, :])
        v1 = _top_values(s1, PEER_TOPK)
        v2, rank2 = _top_values(s2, PEER_TOPK, with_rank=True)
        cands = [v1[i] + v2[j] for i in range(PEER_TOPK) for j in range(PEER_TOPK)
                 if (i + 1) * (j + 1) <= PEER_TOPK]
        pad = (-len(cands)) % SUBLANES
        cands += [jnp.full((1, tm), -jnp.inf, F32)] * pad
        best = _top_values(jnp.concatenate(cands, axis=0), PEER_TOPK)
        tau = best[PEER_TOPK - 1]
        z = sum(jnp.exp(b - best[0]) for b in best)
        n = jnp.zeros(s1.shape, F32)
        for j in range(PEER_TOPK):
            n = n + jnp.where(s1 + v2[j] >= tau, 1.0, 0.0)
        n_ref[h] = _twice_bf16(n)
        e1_ref[h] = _twice_bf16(0.5 * jnp.exp(s1 - v1[0]) / z)
        rank2_ref[h] = rank2.astype(BF16)
        e2_ref[h] = jnp.exp(s2 - v2[0]).astype(BF16)


def _peer_route(xn, wqt, keys, tm):
    t, d = xn.shape
    nk = PEER_N_KEYS
    words = jax.ShapeDtypeStruct((PEER_HEADS, nk, t), jnp.uint32)
    halves = jax.ShapeDtypeStruct((PEER_HEADS, nk, t), BF16)
    ospec = pl.BlockSpec((PEER_HEADS, nk, tm), lambda i: (0, 0, i))
    return pl.pallas_call(
        functools.partial(_route_kernel, tm=tm),
        out_shape=(words, words, halves, halves),
        grid=(t // tm,),
        in_specs=[
            pl.BlockSpec((tm, d), lambda i: (i, 0)),
            pl.BlockSpec(wqt.shape, lambda i: (0, 0)),
            pl.BlockSpec(keys.shape, lambda i: (0, 0, 0)),
        ],
        out_specs=(ospec, ospec, ospec, ospec),
        compiler_params=_params("parallel"),
        name="peer_route",
    )(xn, wqt, keys)


GELU_C0 = math.sqrt(2.0 / math.pi)
GELU_C1 = GELU_C0 * 0.044715


def _peer_dense_kernel(xn_ref, x2_ref, u_ref, vt_ref, n_ref, e1_ref, rank2_ref, e2_ref, o_ref,
                       acc_ref, s_ref, w_ref, *, na):
    j = pl.program_id(1)
    nk = PEER_N_KEYS
    tm = s_ref.shape[1]

    @pl.when(j == 0)
    def _():
        acc_ref[...] = jnp.zeros(acc_ref.shape, F32)

    s_ref[...] = _dot_nt(u_ref[...], xn_ref[...])

    def per_a(a, carry):
        rows = pl.ds(pl.multiple_of(a * nk, nk), nk)
        coef = None
        for h in range(PEER_HEADS):
            n_b = pltpu.bitcast(jnp.broadcast_to(n_ref[h, pl.ds(a, 1), :], (nk // 2, tm)), BF16)
            e_b = pltpu.bitcast(jnp.broadcast_to(e1_ref[h, pl.ds(a, 1), :], (nk // 2, tm)), BF16)
            term = e_b * jnp.where(rank2_ref[h] < n_b, e2_ref[h], jnp.zeros((), BF16))
            coef = term if coef is None else coef + term
        x = s_ref[rows, :]
        th = jnp.tanh(x * (GELU_C0 + GELU_C1 * (x * x)))
        w_ref[rows, :] = coef * (x + x * th).astype(BF16)
        return carry

    lax.fori_loop(0, na, per_a, 0)
    acc_ref[...] += _dot(vt_ref[...], w_ref[...])

    @pl.when(j == pl.num_programs(1) - 1)
    def _():
        o_ref[...] = x2_ref[...] + acc_ref[...].T


def _peer_dense(xn, x2, u_bf, vt_bf, n_w, e1_w, rank2, e2, tm, na):
    t, d = xn.shape
    n_exp = u_bf.shape[0]
    nk = PEER_N_KEYS
    te = na * nk
    return pl.pallas_call(
        functools.partial(_peer_dense_kernel, na=na),
        out_shape=jax.ShapeDtypeStruct((t, d), F32),
        grid=(t // tm, n_exp // te),
        in_specs=[
            pl.BlockSpec((tm, d), lambda i, j: (i, 0)),
            pl.BlockSpec((tm, d), lambda i, j: (i, 0)),
            pl.BlockSpec((te, d), lambda i, j: (j, 0)),
            pl.BlockSpec((d, te), lambda i, j: (0, j)),
            pl.BlockSpec((PEER_HEADS, na, tm), lambda i, j: (0, j, i)),
            pl.BlockSpec((PEER_HEADS, na, tm), lambda i, j: (0, j, i)),
            pl.BlockSpec((PEER_HEADS, nk, tm), lambda i, j: (0, 0, i)),
            pl.BlockSpec((PEER_HEADS, nk, tm), lambda i, j: (0, 0, i)),
        ],
        out_specs=pl.BlockSpec((tm, d), lambda i, j: (i, 0)),
        scratch_shapes=[pltpu.VMEM((d, tm), F32), pltpu.VMEM((te, tm), F32), pltpu.VMEM((te, tm), BF16)],
        compiler_params=_params("parallel", "arbitrary"),
        name="peer_dense",
    )(xn, x2, u_bf, vt_bf, n_w, e1_w, rank2, e2)


def _block_diag_tiles(w):
    nb, r, _ = w.shape
    per = MXU_TILE // r
    w = w.reshape(nb // per, per, r, r)
    eye = jnp.eye(per, dtype=w.dtype)
    return jnp.einsum("gpij,pq->gpiqj", w, eye).reshape(nb // per, MXU_TILE, MXU_TILE)


def _layer(x, lam_init, norm1_g, w_in, b_gate, q_norm_g, k_norm_g, lambda_q1, lambda_k1, lambda_q2,
           lambda_k2, subln_g, conv_w, conv_b, w_rg_a, b_rg_a, w_rg_x, b_rg_x, rg_lambda, w_br_attn,
           w_br_rnn, w_out, norm2_g, w_peer_q, peer_sub_keys, peer_u, peer_v):
    batch, seq, d = x.shape
    t = batch * seq
    d_rnn = conv_w.shape[1]
    x2d = x.reshape(t, d)
    row = lambda v: v.reshape(1, -1)

    w_in_bf = w_in.astype(BF16)
    n_grp = d // HEAD_DIM
    scale = HEAD_DIM ** -0.5
    gn = jnp.stack([jnp.tile(q_norm_g, n_grp) * scale, jnp.tile(k_norm_g, n_grp),
                    jnp.ones((d,), F32)]).reshape(3, 1, d)
    gsum = jnp.kron(jnp.eye(MXU_TILE // HEAD_DIM, dtype=F32),
                    jnp.full((HEAD_DIM, HEAD_DIM), 1.0 / HEAD_DIM, F32)).astype(BF16)
    wxy = w_in_bf[:, 3 * d:3 * d + 2 * d_rnn]
    wg = w_in_bf[:, 3 * d + 2 * d_rnn:]
    wa_bd = _block_diag_tiles(w_rg_a).astype(BF16)
    wx_bd = _block_diag_tiles(w_rg_x).astype(BF16)
    wqt = w_peer_q.T.astype(BF16)
    keys = peer_sub_keys.reshape(PEER_HEADS * 2, PEER_N_KEYS, -1).astype(BF16)
    u_bf = peer_u.astype(BF16)
    vt_bf = peer_v.T.astype(BF16)

    g1 = row(norm1_g)
    qkv = _qkv_proj(x2d, g1, w_in_bf, gn, gsum, tm=512)
    attn = _diff_attention(qkv, row(lambda_q1), row(lambda_k1), row(lambda_q2), row(lambda_k2),
                           row(subln_g), batch, seq, lam_init, tq=512)
    rnn = _rglru(x2d, g1, wxy, conv_w, row(conv_b), wa_bd, row(b_rg_a), wx_bd, row(b_rg_x),
                 row(rg_lambda), batch, seq, ts=256)
    x2, xn = _merge(x2d, attn, rnn, g1, wg, row(b_gate), w_br_attn.astype(BF16), w_br_rnn.astype(BF16),
                    w_out.astype(BF16), row(norm2_g), tm=256)
    n_w, e1_w, rank2, e2 = _peer_route(xn, wqt, keys, tm=256)
    out = _peer_dense(xn, x2, u_bf, vt_bf, n_w, e1_w, rank2, e2, tm=512, na=8)
    return out.reshape(batch, seq, d)


def kernel(x, norm1_g, w_in, b_gate, q_norm_g, k_norm_g, lambda_q1, lambda_k1, lambda_q2, lambda_k2,
           subln_g, conv_w, conv_b, w_rg_a, b_rg_a, w_rg_x, b_rg_x, rg_lambda, w_br_attn, w_br_rnn,
           w_out, norm2_g, w_peer_q, peer_sub_keys, peer_u, peer_v):
    depth = norm1_g.shape[0]
    for layer in range(depth):
        lam_init = 0.8 - 0.6 * math.exp(-0.3 * layer)
        x = _layer(x, lam_init, norm1_g[layer], w_in[layer], b_gate[layer], q_norm_g[layer],
                   k_norm_g[layer], lambda_q1[layer], lambda_k1[layer], lambda_q2[layer],
                   lambda_k2[layer], subln_g[layer], conv_w[layer], conv_b[layer], w_rg_a[layer],
                   b_rg_a[layer], w_rg_x[layer], b_rg_x[layer], rg_lambda[layer], w_br_attn[layer],
                   w_br_rnn[layer], w_out[layer], norm2_g[layer], w_peer_q[layer], peer_sub_keys[layer],
                   peer_u[layer], peer_v[layer])
    return x
```

```python
import functools
import math

import jax
import jax.numpy as jnp
from jax import lax
from jax.experimental import pallas as pl
from jax.experimental.pallas import tpu as pltpu

F32 = jnp.float32
BF16 = jnp.bfloat16

EPS = 1e-6
NEG_INF = -1e30
LOG2_E = math.log2(math.e)
CHUNK = 64
N_HEADS = 8
HEAD_DIM = 64
V_HEAD_DIM = 128
RNN_BLOCK = 64
CONV_WIDTH = 4
RG_C = 8.0
PEER_HEADS = 8
PEER_N_KEYS = 128
PEER_TOPK = 16

MXU_TILE = 256
SUBLANES = 8
VMEM_LIMIT = 56 * 1024 * 1024


def _params(*sem):
    return pltpu.CompilerParams(dimension_semantics=sem, vmem_limit_bytes=VMEM_LIMIT)


def _rms(x, g):
    ms = jnp.mean(x * x, axis=-1, keepdims=True)
    return x * lax.rsqrt(ms + EPS) * g


def _dot(a, b):
    return jnp.dot(a, b, preferred_element_type=F32)


def _dot_nt(a, b):
    return lax.dot_general(a, b, (((1,), (1,)), ((), ())), preferred_element_type=F32)


def _qkv_kernel(x_ref, g1_ref, w_ref, gn_ref, gsum_ref, o_ref, h_ref):
    j = pl.program_id(1)

    @pl.when(j == 0)
    def _():
        h_ref[...] = _rms(x_ref[...], g1_ref[...]).astype(BF16)

    y = _dot(h_ref[...], w_ref[...])

    @pl.when(j < 2)
    def _():
        y2 = y * y
        hi = y2.astype(BF16)
        lo = (y2 - hi.astype(F32)).astype(BF16)
        g = gsum_ref[...]
        ms = jnp.concatenate(
            [_dot(hi[:, c:c + MXU_TILE], g) + _dot(lo[:, c:c + MXU_TILE], g)
             for c in range(0, y.shape[1], MXU_TILE)], axis=-1)
        o_ref[...] = (y * lax.rsqrt(ms + EPS) * gn_ref[0]).astype(BF16)

    @pl.when(j == 2)
    def _():
        o_ref[...] = y.astype(BF16)


def _qkv_proj(x2d, g1, w_in_bf, gn, gsum, tm):
    t, d = x2d.shape
    return pl.pallas_call(
        _qkv_kernel,
        out_shape=jax.ShapeDtypeStruct((t, 3 * d), BF16),
        grid=(t // tm, 3),
        in_specs=[
            pl.BlockSpec((tm, d), lambda i, j: (i, 0)),
            pl.BlockSpec((1, d), lambda i, j: (0, 0)),
            pl.BlockSpec((d, d), lambda i, j: (0, j)),
            pl.BlockSpec((1, 1, d), lambda i, j: (j, 0, 0)),
            pl.BlockSpec((MXU_TILE, MXU_TILE), lambda i, j: (0, 0)),
        ],
        out_specs=pl.BlockSpec((tm, d), lambda i, j: (i, j)),
        scratch_shapes=[pltpu.VMEM((tm, d), BF16)],
        compiler_params=_params("parallel", "arbitrary"),
        name="qkv_proj",
    )(x2d, g1, w_in_bf, gn, gsum)


def _attn_kernel(q_ref, k_ref, v_ref, lq1_ref, lk1_ref, lq2_ref, lk2_ref, sg_ref, o_ref,
                 s_buf, mrun_ref, m_ref, acc_ref, vx_ref, bias_ref, *, tq, lam_init):
    i = pl.program_id(2)
    hd = V_HEAD_DIM

    @pl.when((pl.program_id(0) == 0) & (pl.program_id(1) == 0) & (i == 0))
    def _():
        r = lax.broadcasted_iota(jnp.int32, bias_ref.shape, 0)
        c = lax.broadcasted_iota(jnp.int32, bias_ref.shape, 1)
        qpos = jnp.where(r >= tq, r - tq, r)
        bias_ref[...] = jnp.where((c // CHUNK) <= (qpos // CHUNK), 0.0, NEG_INF)

    @pl.when(i == 0)
    def _():
        vx_ref[:, :hd] = v_ref[...]
        vx_ref[:, hd:] = jnp.ones((vx_ref.shape[0], hd), BF16)

    q = q_ref[...]
    lane = lax.broadcasted_iota(jnp.int32, q.shape, 1)
    zero = jnp.zeros_like(q)
    qs = jnp.concatenate([jnp.where(lane < HEAD_DIM, q, zero), jnp.where(lane >= HEAD_DIM, q, zero)], axis=0)

    mrun_ref[...] = jnp.full(mrun_ref.shape, NEG_INF, F32)

    def scores(j, masked):
        start = pl.multiple_of(j * tq, tq)
        s = _dot_nt(qs, k_ref[pl.ds(start, tq), :])
        if masked:
            s = s + bias_ref[...]
        s_buf[j] = s
        m = mrun_ref[...]
        for c0 in range(0, tq, hd):
            m = jnp.maximum(m, s[:, c0:c0 + hd])
        mrun_ref[...] = m

    def full_scores(j, carry):
        scores(j, False)
        return carry

    lax.fori_loop(0, i, full_scores, 0)
    scores(i, True)
    m_ref[...] = jnp.broadcast_to(jnp.max(mrun_ref[...], axis=-1, keepdims=True), m_ref.shape)

    acc_ref[...] = jnp.zeros(acc_ref.shape, F32)

    def weighted(j, carry):
        start = pl.multiple_of(j * tq, tq)
        s = s_buf[j]
        m = m_ref[...]
        p = jnp.concatenate([jnp.exp2(s[:, c0:c0 + hd] - m) for c0 in range(0, tq, hd)], axis=-1)
        acc_ref[...] += _dot(p.astype(BF16), vx_ref[pl.ds(start, tq), :])
        return carry

    lax.fori_loop(0, i + 1, weighted, 0)

    lam = (jnp.exp(jnp.sum(lq1_ref[...] * lk1_ref[...], axis=-1, keepdims=True))
           - jnp.exp(jnp.sum(lq2_ref[...] * lk2_ref[...], axis=-1, keepdims=True)) + lam_init)
    o = acc_ref[:, :hd] / acc_ref[:, hd:]
    o = o[:tq] - lam * o[tq:]
    o_ref[...] = (_rms(o, sg_ref[...]) * (1.0 - lam_init)).astype(BF16)


def _diff_attention(qkv, lq1, lk1, lq2, lk2, subln_g, batch, seq, lam_init, tq):
    t = qkv.shape[0]
    nq = seq // tq
    hd = V_HEAD_DIM
    small = pl.BlockSpec((1, HEAD_DIM), lambda b, h, i: (0, 0))
    return pl.pallas_call(
        functools.partial(_attn_kernel, tq=tq, lam_init=lam_init),
        out_shape=jax.ShapeDtypeStruct((t, N_HEADS * hd), BF16),
        grid=(batch, N_HEADS, nq),
        in_specs=[
            pl.BlockSpec((tq, hd), lambda b, h, i: (b * nq + i, h)),
            pl.BlockSpec((seq, hd), lambda b, h, i: (b, N_HEADS + h)),
            pl.BlockSpec((seq, hd), lambda b, h, i: (b, 2 * N_HEADS + h)),
            small, small, small, small,
            pl.BlockSpec((1, hd), lambda b, h, i: (0, 0)),
        ],
        out_specs=pl.BlockSpec((tq, hd), lambda b, h, i: (b * nq + i, h)),
        scratch_shapes=[pltpu.VMEM((nq, 2 * tq, tq), F32), pltpu.VMEM((2 * tq, hd), F32),
                        pltpu.VMEM((2 * tq, hd), F32), pltpu.VMEM((2 * tq, 2 * hd), F32),
                        pltpu.VMEM((seq, 2 * hd), BF16), pltpu.VMEM((2 * tq, tq), F32)],
        compiler_params=_params("arbitrary", "arbitrary", "arbitrary"),
        name="diff_attn",
    )(qkv, qkv, qkv, lq1, lk1, lq2, lk2, subln_g)


def _rglru_kernel(x_ref, g1_ref, wxy_ref, cw_ref, cb_ref, wa_ref, ba_ref, wx_ref, bx_ref, lam_ref, o_ref,
                  xbuf_ref, h_ref, *, ts, d_rnn):
    s_idx = pl.program_id(1)

    @pl.when(s_idx == 0)
    def _():
        xbuf_ref[0:SUBLANES, :] = jnp.zeros((SUBLANES, d_rnn), F32)
        h_ref[...] = jnp.zeros(h_ref.shape, F32)

    h_in = _rms(x_ref[...], g1_ref[...]).astype(BF16)
    xy = _dot(h_in, wxy_ref[...])
    xr = xy[:, :d_rnn]
    yr = xy[:, d_rnn:]

    xbuf_ref[SUBLANES:SUBLANES + ts, :] = xr
    conv = cb_ref[...] + jnp.zeros((ts, d_rnn), F32)
    for tap in range(CONV_WIDTH):
        off = SUBLANES - (CONV_WIDTH - 1) + tap
        conv = conv + xbuf_ref[off:off + ts, :] * cw_ref[tap:tap + 1, :]
    xbuf_ref[0:SUBLANES, :] = xbuf_ref[ts:ts + SUBLANES, :]

    cbf = conv.astype(BF16)
    ga, gi = [], []
    for g in range(d_rnn // MXU_TILE):
        sl = slice(g * MXU_TILE, (g + 1) * MXU_TILE)
        ga.append(_dot(cbf[:, sl], wa_ref[g]))
        gi.append(_dot(cbf[:, sl], wx_ref[g]))
    gate_r = jax.nn.sigmoid(jnp.concatenate(ga, axis=-1) + ba_ref[...])
    gate_i = jax.nn.sigmoid(jnp.concatenate(gi, axis=-1) + bx_ref[...])

    neg_lam = -lam_ref[...]
    softplus = jnp.maximum(neg_lam, 0.0) + jnp.log1p(jnp.exp(-jnp.abs(neg_lam)))
    log_a = -RG_C * gate_r * softplus
    a = jnp.exp(log_a)
    th = jnp.tanh(log_a)
    mult = jnp.sqrt(-2.0 * th / (1.0 - th))
    u = mult * (gate_i * conv)

    row = lax.broadcasted_iota(jnp.int32, (ts, d_rnn), 0)
    d = 1
    while d < ts:
        keep = row >= d
        a_sh = jnp.where(keep, pltpu.roll(a, d, 0), 1.0)
        u_sh = jnp.where(keep, pltpu.roll(u, d, 0), 0.0)
        u = a * u_sh + u
        a = a * a_sh
        d *= 2
    hs = u + a * h_ref[...]
    h_ref[...] = hs[ts - 1:ts, :]
    o_ref[...] = (hs * jax.nn.gelu(yr)).astype(BF16)


def _rglru(x2d, g1, wxy, conv_w, conv_b, wa_bd, b_a, wx_bd, b_x, rg_lambda, batch, seq, ts):
    t, d = x2d.shape
    d_rnn = conv_w.shape[1]
    ns = seq // ts
    ng = d_rnn // MXU_TILE
    const2 = lambda b, s: (0, 0)
    const3 = lambda b, s: (0, 0, 0)
    return pl.pallas_call(
        functools.partial(_rglru_kernel, ts=ts, d_rnn=d_rnn),
        out_shape=jax.ShapeDtypeStruct((t, d_rnn), BF16),
        grid=(batch, ns),
        in_specs=[
            pl.BlockSpec((ts, d), lambda b, s: (b * ns + s, 0)),
            pl.BlockSpec((1, d), const2),
            pl.BlockSpec((d, 2 * d_rnn), const2),
            pl.BlockSpec((CONV_WIDTH, d_rnn), const2),
            pl.BlockSpec((1, d_rnn), const2),
            pl.BlockSpec((ng, MXU_TILE, MXU_TILE), const3),
            pl.BlockSpec((1, d_rnn), const2),
            pl.BlockSpec((ng, MXU_TILE, MXU_TILE), const3),
            pl.BlockSpec((1, d_rnn), const2),
            pl.BlockSpec((1, d_rnn), const2),
        ],
        out_specs=pl.BlockSpec((ts, d_rnn), lambda b, s: (b * ns + s, 0)),
        scratch_shapes=[pltpu.VMEM((ts + SUBLANES, d_rnn), F32), pltpu.VMEM((1, d_rnn), F32)],
        compiler_params=_params("parallel", "arbitrary"),
        name="rglru",
    )(x2d, g1, wxy, conv_w, conv_b, wa_bd, b_a, wx_bd, b_x, rg_lambda)


def _merge_kernel(x_ref, attn_ref, rnn_ref, g1_ref, wg_ref, bg_ref, pa_ref, pr_ref, wo_ref, g2_ref,
                  x2_ref, xn_ref, *, d):
    x = x_ref[...]
    h = _rms(x, g1_ref[...]).astype(BF16)
    gates = jax.nn.sigmoid(_dot(h, wg_ref[...]) + bg_ref[...])
    merged = gates[:, :d] * _dot(attn_ref[...], pa_ref[...]) + gates[:, d:] * _dot(rnn_ref[...], pr_ref[...])
    x2 = x + _dot(merged.astype(BF16), wo_ref[...])
    x2_ref[...] = x2
    xn_ref[...] = _rms(x2, g2_ref[...]).T.astype(BF16)


def _merge(x2d, attn, rnn, g1, wg, bg, pa, pr, wo, g2, tm):
    t, d = x2d.shape
    d_rnn = rnn.shape[1]
    const = lambda i: (0, 0)
    row = lambda i: (i, 0)
    return pl.pallas_call(
        functools.partial(_merge_kernel, d=d),
        out_shape=(jax.ShapeDtypeStruct((t, d), F32), jax.ShapeDtypeStruct((d, t), BF16)),
        grid=(t // tm,),
        in_specs=[
            pl.BlockSpec((tm, d), row), pl.BlockSpec((tm, d), row), pl.BlockSpec((tm, d_rnn), row),
            pl.BlockSpec((1, d), const), pl.BlockSpec((d, 2 * d), const), pl.BlockSpec((1, 2 * d), const),
            pl.BlockSpec((d, d), const), pl.BlockSpec((d_rnn, d), const), pl.BlockSpec((d, d), const),
            pl.BlockSpec((1, d), const),
        ],
        out_specs=(pl.BlockSpec((tm, d), row), pl.BlockSpec((d, tm), lambda i: (0, i))),
        compiler_params=_params("parallel"),
        name="merge",
    )(x2d, attn, rnn, g1, wg, bg, pa, pr, wo, g2)


def _top_values(s, k, with_rank=False):
    vals = []
    cur = s
    rank = jnp.full(s.shape, float(k), F32) if with_rank else None
    for r in range(k):
        m = jnp.max(cur, axis=0, keepdims=True)
        vals.append(m)
        eq = cur == m
        if with_rank:
            rank = jnp.where(eq, float(r), rank)
        if r + 1 < k:
            cur = jnp.where(eq, -jnp.inf, cur)
    return (vals, rank) if with_rank else vals


def _route_kernel(xn_ref, wqt_ref, keys_ref, n_ref, e1_ref, rank2_ref, e2_ref, *, tm):
    qt = _dot(wqt_ref[...], xn_ref[...]).astype(BF16)
    nk = PEER_N_KEYS
    for h in range(PEER_HEADS):
        s1 = _dot(keys_ref[2 * h], qt[(2 * h) * nk:(2 * h + 1) * nk, :])
        s2 = _dot(keys_ref[2 * h + 1], qt[(2 * h + 1) * nk:(2 * h + 2) * nk, :])
        v1 = _top_values(s1, PEER_TOPK)
        v2, rank2 = _top_values(s2, PEER_TOPK, with_rank=True)
        cands = [v1[i] + v2[j] for i in range(PEER_TOPK) for j in range(PEER_TOPK)
                 if (i + 1) * (j + 1) <= PEER_TOPK]
        pad = (-len(cands)) % SUBLANES
        cands += [jnp.full((1, tm), -jnp.inf, F32)] * pad
        best = _top_values(jnp.concatenate(cands, axis=0), PEER_TOPK)
        tau = best[PEER_TOPK - 1]
        z = sum(jnp.exp(b - best[0]) for b in best)
        n = jnp.zeros(s1.shape, F32)
        for j in range(PEER_TOPK):
            n = n + jnp.where(s1 + v2[j] >= tau, 1.0, 0.0)
        n_ref[h] = n
        e1_ref[h] = 0.5 * jnp.exp(s1 - v1[0]) / z
        rank2_ref[h] = rank2.astype(BF16)
        e2_ref[h] = jnp.exp(s2 - v2[0]).astype(BF16)


def _peer_route(xn, wqt, keys, tm):
    d, t = xn.shape
    nk = PEER_N_KEYS
    words = jax.ShapeDtypeStruct((PEER_HEADS, nk, t), F32)
    halves = jax.ShapeDtypeStruct((PEER_HEADS, nk, t), BF16)
    ospec = pl.BlockSpec((PEER_HEADS, nk, tm), lambda i: (0, 0, i))
    return pl.pallas_call(
        functools.partial(_route_kernel, tm=tm),
        out_shape=(words, words, halves, halves),
        grid=(t // tm,),
        in_specs=[
            pl.BlockSpec((d, tm), lambda i: (0, i)),
            pl.BlockSpec(wqt.shape, lambda i: (0, 0)),
            pl.BlockSpec(keys.shape, lambda i: (0, 0, 0)),
        ],
        out_specs=(ospec, ospec, ospec, ospec),
        compiler_params=_params("parallel"),
        name="peer_route",
    )(xn, wqt, keys)


GELU_C0 = math.sqrt(2.0 / math.pi)
GELU_C1 = GELU_C0 * 0.044715


def _peer_dense_kernel(xn_ref, x2_ref, u_ref, vt_ref, n_ref, e1_ref, rank2_ref, e2_ref, o_ref,
                       acc_ref, s_ref, w_ref, *, na):
    j = pl.program_id(1)
    nk = PEER_N_KEYS
    tm = s_ref.shape[1]

    @pl.when(j == 0)
    def _():
        acc_ref[...] = jnp.zeros(acc_ref.shape, F32)

    s_ref[...] = _dot(u_ref[...], xn_ref[...])

    for a in range(na):
        rows = slice(a * nk, (a + 1) * nk)
        coef = None
        for h in range(PEER_HEADS):
            n_b = jnp.broadcast_to(n_ref[h, a:a + 1, :].astype(BF16), (nk, tm))
            e_b = jnp.broadcast_to(e1_ref[h, a:a + 1, :].astype(BF16), (nk, tm))
            term = e_b * jnp.where(rank2_ref[h] < n_b, e2_ref[h], jnp.zeros((), BF16))
            coef = term if coef is None else coef + term
        x = s_ref[rows, :]
        th = jnp.tanh(x * (GELU_C0 + GELU_C1 * (x * x)))
        w_ref[rows, :] = coef * (x + x * th).astype(BF16)
    acc_ref[...] += _dot(vt_ref[...], w_ref[...])

    @pl.when(j == pl.num_programs(1) - 1)
    def _():
        o_ref[...] = x2_ref[...] + acc_ref[...].T


def _peer_dense(xn, x2, u_bf, vt_bf, n_w, e1_w, rank2, e2, tm, na):
    d, t = xn.shape
    n_exp = u_bf.shape[0]
    nk = PEER_N_KEYS
    te = na * nk
    return pl.pallas_call(
        functools.partial(_peer_dense_kernel, na=na),
        out_shape=jax.ShapeDtypeStruct((t, d), F32),
        grid=(t // tm, n_exp // te),
        in_specs=[
            pl.BlockSpec((d, tm), lambda i, j: (0, i)),
            pl.BlockSpec((tm, d), lambda i, j: (i, 0)),
            pl.BlockSpec((te, d), lambda i, j: (j, 0)),
            pl.BlockSpec((d, te), lambda i, j: (0, j)),
            pl.BlockSpec((PEER_HEADS, na, tm), lambda i, j: (0, j, i)),
            pl.BlockSpec((PEER_HEADS, na, tm), lambda i, j: (0, j, i)),
            pl.BlockSpec((PEER_HEADS, nk, tm), lambda i, j: (0, 0, i)),
            pl.BlockSpec((PEER_HEADS, nk, tm), lambda i, j: (0, 0, i)),
        ],
        out_specs=pl.BlockSpec((tm, d), lambda i, j: (i, 0)),
        scratch_shapes=[pltpu.VMEM((d, tm), F32), pltpu.VMEM((te, tm), F32), pltpu.VMEM((te, tm), BF16)],
        compiler_params=_params("parallel", "arbitrary"),
        name="peer_dense",
    )(xn, x2, u_bf, vt_bf, n_w, e1_w, rank2, e2)


def _block_diag_tiles(w):
    nb, r, _ = w.shape
    per = MXU_TILE // r
    w = w.reshape(nb // per, per, r, r)
    eye = jnp.eye(per, dtype=w.dtype)
    return jnp.einsum("gpij,pq->gpiqj", w, eye).reshape(nb // per, MXU_TILE, MXU_TILE)


def _layer(x, lam_init, norm1_g, w_in, b_gate, q_norm_g, k_norm_g, lambda_q1, lambda_k1, lambda_q2,
           lambda_k2, subln_g, conv_w, conv_b, w_rg_a, b_rg_a, w_rg_x, b_rg_x, rg_lambda, w_br_attn,
           w_br_rnn, w_out, norm2_g, w_peer_q, peer_sub_keys, peer_u, peer_v):
    batch, seq, d = x.shape
    t = batch * seq
    d_rnn = conv_w.shape[1]
    x2d = x.reshape(t, d)
    row = lambda v: v.reshape(1, -1)

    w_in_bf = w_in.astype(BF16)
    n_grp = d // HEAD_DIM
    scale = HEAD_DIM ** -0.5
    gn = jnp.stack([jnp.tile(q_norm_g, n_grp) * (scale * LOG2_E), jnp.tile(k_norm_g, n_grp),
                    jnp.ones((d,), F32)]).reshape(3, 1, d)
    gsum = jnp.kron(jnp.eye(MXU_TILE // HEAD_DIM, dtype=F32),
                    jnp.full((HEAD_DIM, HEAD_DIM), 1.0 / HEAD_DIM, F32)).astype(BF16)
    wxy = w_in_bf[:, 3 * d:3 * d + 2 * d_rnn]
    wg = w_in_bf[:, 3 * d + 2 * d_rnn:]
    wa_bd = _block_diag_tiles(w_rg_a).astype(BF16)
    wx_bd = _block_diag_tiles(w_rg_x).astype(BF16)
    wqt = w_peer_q.T.astype(BF16)
    keys = peer_sub_keys.reshape(PEER_HEADS * 2, PEER_N_KEYS, -1).astype(BF16)
    u_bf = peer_u.astype(BF16)
    vt_bf = peer_v.T.astype(BF16)

    g1 = row(norm1_g)
    qkv = _qkv_proj(x2d, g1, w_in_bf, gn, gsum, tm=512)
    attn = _diff_attention(qkv, row(lambda_q1), row(lambda_k1), row(lambda_q2), row(lambda_k2),
                           row(subln_g), batch, seq, lam_init, tq=512)
    rnn = _rglru(x2d, g1, wxy, conv_w, row(conv_b), wa_bd, row(b_rg_a), wx_bd, row(b_rg_x),
                 row(rg_lambda), batch, seq, ts=256)
    x2, xn = _merge(x2d, attn, rnn, g1, wg, row(b_gate), w_br_attn.astype(BF16), w_br_rnn.astype(BF16),
                    w_out.astype(BF16), row(norm2_g), tm=256)
    n_w, e1_w, rank2, e2 = _peer_route(xn, wqt, keys, tm=256)
    out = _peer_dense(xn, x2, u_bf, vt_bf, n_w, e1_w, rank2, e2, tm=512, na=8)
    return out.reshape(batch, seq, d)


def kernel(x, norm1_g, w_in, b_gate, q_norm_g, k_norm_g, lambda_q1, lambda_k1, lambda_q2, lambda_k2,
           subln_g, conv_w, conv_b, w_rg_a, b_rg_a, w_rg_x, b_rg_x, rg_lambda, w_br_attn, w_br_rnn,
           w_out, norm2_g, w_peer_q, peer_sub_keys, peer_u, peer_v):
    depth = norm1_g.shape[0]
    for layer in range(depth):
        lam_init = 0.8 - 0.6 * math.exp(-0.3 * layer)
        x = _layer(x, lam_init, norm1_g[layer], w_in[layer], b_gate[layer], q_norm_g[layer],
                   k_norm_g[layer], lambda_q1[layer], lambda_k1[layer], lambda_q2[layer],
                   lambda_k2[layer], subln_g[layer], conv_w[layer], conv_b[layer], w_rg_a[layer],
                   b_rg_a[layer], w_rg_x[layer], b_rg_x[layer], rg_lambda[layer], w_br_attn[layer],
                   w_br_rnn[layer], w_out[layer], norm2_g[layer], w_peer_q[layer], peer_sub_keys[layer],
                   peer_u[layer], peer_v[layer])
    return x
```

```python
import functools
import math

import jax
import jax.numpy as jnp
from jax import lax
from jax.experimental import pallas as pl
from jax.experimental.pallas import tpu as pltpu

F32 = jnp.float32
BF16 = jnp.bfloat16

EPS = 1e-6
NEG_INF = -1e30
LOG2_E = math.log2(math.e)
CHUNK = 64
N_HEADS = 8
HEAD_DIM = 64
V_HEAD_DIM = 128
RNN_BLOCK = 64
CONV_WIDTH = 4
RG_C = 8.0
PEER_HEADS = 8
PEER_N_KEYS = 128
PEER_TOPK = 16

MXU_TILE = 256
SUBLANES = 8
VMEM_LIMIT = 56 * 1024 * 1024


def _params(*sem):
    return pltpu.CompilerParams(dimension_semantics=sem, vmem_limit_bytes=VMEM_LIMIT)


def _rms(x, g):
    ms = jnp.mean(x * x, axis=-1, keepdims=True)
    return x * lax.rsqrt(ms + EPS) * g


def _dot(a, b):
    return jnp.dot(a, b, preferred_element_type=F32)


def _dot_nt(a, b):
    return lax.dot_general(a, b, (((1,), (1,)), ((), ())), preferred_element_type=F32)


def _qkv_kernel(x_ref, g1_ref, w_ref, gn_ref, gsum_ref, o_ref, h_ref):
    j = pl.program_id(1)

    @pl.when(j == 0)
    def _():
        h_ref[...] = _rms(x_ref[...], g1_ref[...]).astype(BF16)

    y = _dot(h_ref[...], w_ref[...])

    @pl.when(j < 2)
    def _():
        y2 = y * y
        hi = y2.astype(BF16)
        lo = (y2 - hi.astype(F32)).astype(BF16)
        g = gsum_ref[...]
        ms = jnp.concatenate(
            [_dot(hi[:, c:c + MXU_TILE], g) + _dot(lo[:, c:c + MXU_TILE], g)
             for c in range(0, y.shape[1], MXU_TILE)], axis=-1)
        o_ref[...] = (y * lax.rsqrt(ms + EPS) * gn_ref[0]).astype(BF16)

    @pl.when(j == 2)
    def _():
        o_ref[...] = y.astype(BF16)


def _qkv_proj(x2d, g1, w_in_bf, gn, gsum, tm):
    t, d = x2d.shape
    return pl.pallas_call(
        _qkv_kernel,
        out_shape=jax.ShapeDtypeStruct((t, 3 * d), BF16),
        grid=(t // tm, 3),
        in_specs=[
            pl.BlockSpec((tm, d), lambda i, j: (i, 0)),
            pl.BlockSpec((1, d), lambda i, j: (0, 0)),
            pl.BlockSpec((d, d), lambda i, j: (0, j)),
            pl.BlockSpec((1, 1, d), lambda i, j: (j, 0, 0)),
            pl.BlockSpec((MXU_TILE, MXU_TILE), lambda i, j: (0, 0)),
        ],
        out_specs=pl.BlockSpec((tm, d), lambda i, j: (i, j)),
        scratch_shapes=[pltpu.VMEM((tm, d), BF16)],
        compiler_params=_params("parallel", "arbitrary"),
        name="qkv_proj",
    )(x2d, g1, w_in_bf, gn, gsum)


def _attn_kernel(q_ref, k_ref, v_ref, lq1_ref, lk1_ref, lq2_ref, lk2_ref, sg_ref, o_ref,
                 s_buf, mrun_ref, m_ref, acc_ref, vx_ref, bias_ref, *, tq, lam_init):
    i = pl.program_id(2)
    hd = V_HEAD_DIM

    @pl.when((pl.program_id(0) == 0) & (pl.program_id(1) == 0) & (i == 0))
    def _():
        r = lax.broadcasted_iota(jnp.int32, bias_ref.shape, 0)
        c = lax.broadcasted_iota(jnp.int32, bias_ref.shape, 1)
        qpos = jnp.where(r >= tq, r - tq, r)
        bias_ref[...] = jnp.where((c // CHUNK) <= (qpos // CHUNK), 0.0, NEG_INF)

    @pl.when(i == 0)
    def _():
        vx_ref[:, :hd] = v_ref[...]
        vx_ref[:, hd:] = jnp.ones((vx_ref.shape[0], hd), BF16)

    q = q_ref[...]
    lane = lax.broadcasted_iota(jnp.int32, q.shape, 1)
    zero = jnp.zeros_like(q)
    qs = jnp.concatenate([jnp.where(lane < HEAD_DIM, q, zero), jnp.where(lane >= HEAD_DIM, q, zero)], axis=0)

    mrun_ref[...] = jnp.full(mrun_ref.shape, NEG_INF, F32)

    def scores(j, masked):
        start = pl.multiple_of(j * tq, tq)
        s = _dot_nt(qs, k_ref[pl.ds(start, tq), :])
        if masked:
            s = s + bias_ref[...]
        s_buf[j] = s
        m = mrun_ref[...]
        for c0 in range(0, tq, hd):
            m = jnp.maximum(m, s[:, c0:c0 + hd])
        mrun_ref[...] = m

    def full_scores(p, carry):
        scores(2 * p, False)
        scores(2 * p + 1, False)
        return carry

    lax.fori_loop(0, i // 2, full_scores, 0)

    @pl.when(i % 2 == 1)
    def _():
        scores(i - 1, False)

    scores(i, True)
    m_ref[...] = jnp.broadcast_to(jnp.max(mrun_ref[...], axis=-1, keepdims=True), m_ref.shape)

    acc_ref[...] = jnp.zeros(acc_ref.shape, F32)

    def weighted(j):
        start = pl.multiple_of(j * tq, tq)
        s = s_buf[j]
        m = m_ref[...]
        p = jnp.concatenate([jnp.exp2(s[:, c0:c0 + hd] - m) for c0 in range(0, tq, hd)], axis=-1)
        return _dot(p.astype(BF16), vx_ref[pl.ds(start, tq), :])

    def pair_weighted(p, carry):
        acc_ref[...] += weighted(2 * p) + weighted(2 * p + 1)
        return carry

    lax.fori_loop(0, (i + 1) // 2, pair_weighted, 0)

    @pl.when(i % 2 == 0)
    def _():
        acc_ref[...] += weighted(i)

    lam = (jnp.exp(jnp.sum(lq1_ref[...] * lk1_ref[...], axis=-1, keepdims=True))
           - jnp.exp(jnp.sum(lq2_ref[...] * lk2_ref[...], axis=-1, keepdims=True)) + lam_init)
    o = acc_ref[:, :hd] / acc_ref[:, hd:]
    o = o[:tq] - lam * o[tq:]
    o_ref[...] = (_rms(o, sg_ref[...]) * (1.0 - lam_init)).astype(BF16)


def _diff_attention(qkv, lq1, lk1, lq2, lk2, subln_g, batch, seq, lam_init, tq):
    t = qkv.shape[0]
    nq = seq // tq
    hd = V_HEAD_DIM
    small = pl.BlockSpec((1, HEAD_DIM), lambda b, h, i: (0, 0))
    return pl.pallas_call(
        functools.partial(_attn_kernel, tq=tq, lam_init=lam_init),
        out_shape=jax.ShapeDtypeStruct((t, N_HEADS * hd), BF16),
        grid=(batch, N_HEADS, nq),
        in_specs=[
            pl.BlockSpec((tq, hd), lambda b, h, i: (b * nq + i, h)),
            pl.BlockSpec((seq, hd), lambda b, h, i: (b, N_HEADS + h)),
            pl.BlockSpec((seq, hd), lambda b, h, i: (b, 2 * N_HEADS + h)),
            small, small, small, small,
            pl.BlockSpec((1, hd), lambda b, h, i: (0, 0)),
        ],
        out_specs=pl.BlockSpec((tq, hd), lambda b, h, i: (b * nq + i, h)),
        scratch_shapes=[pltpu.VMEM((nq, 2 * tq, tq), F32), pltpu.VMEM((2 * tq, hd), F32),
                        pltpu.VMEM((2 * tq, hd), F32), pltpu.VMEM((2 * tq, 2 * hd), F32),
                        pltpu.VMEM((seq, 2 * hd), BF16), pltpu.VMEM((2 * tq, tq), F32)],
        compiler_params=_params("arbitrary", "arbitrary", "arbitrary"),
        name="diff_attn",
    )(qkv, qkv, qkv, lq1, lk1, lq2, lk2, subln_g)


def _rglru_kernel(x_ref, g1_ref, wxy_ref, cw_ref, cb_ref, wa_ref, ba_ref, wx_ref, bx_ref, lam_ref, o_ref,
                  xbuf_ref, h_ref, *, ts, d_rnn):
    s_idx = pl.program_id(1)

    @pl.when(s_idx == 0)
    def _():
        xbuf_ref[0:SUBLANES, :] = jnp.zeros((SUBLANES, d_rnn), F32)
        h_ref[...] = jnp.zeros(h_ref.shape, F32)

    h_in = _rms(x_ref[...], g1_ref[...]).astype(BF16)
    xy = _dot(h_in, wxy_ref[...])
    xr = xy[:, :d_rnn]
    yr = xy[:, d_rnn:]

    xbuf_ref[SUBLANES:SUBLANES + ts, :] = xr
    conv = cb_ref[...] + jnp.zeros((ts, d_rnn), F32)
    for tap in range(CONV_WIDTH):
        off = SUBLANES - (CONV_WIDTH - 1) + tap
        conv = conv + xbuf_ref[off:off + ts, :] * cw_ref[tap:tap + 1, :]
    xbuf_ref[0:SUBLANES, :] = xbuf_ref[ts:ts + SUBLANES, :]

    cbf = conv.astype(BF16)
    ga, gi = [], []
    for g in range(d_rnn // MXU_TILE):
        sl = slice(g * MXU_TILE, (g + 1) * MXU_TILE)
        ga.append(_dot(cbf[:, sl], wa_ref[g]))
        gi.append(_dot(cbf[:, sl], wx_ref[g]))
    gate_r = jax.nn.sigmoid(jnp.concatenate(ga, axis=-1) + ba_ref[...])
    gate_i = jax.nn.sigmoid(jnp.concatenate(gi, axis=-1) + bx_ref[...])

    neg_lam = -lam_ref[...]
    softplus = jnp.maximum(neg_lam, 0.0) + jnp.log1p(jnp.exp(-jnp.abs(neg_lam)))
    log_a = -RG_C * gate_r * softplus
    a = jnp.exp(log_a)
    th = jnp.tanh(log_a)
    mult = jnp.sqrt(-2.0 * th / (1.0 - th))
    u = mult * (gate_i * conv)

    groups = ts // SUBLANES
    a3 = a.reshape(groups, SUBLANES, d_rnn)
    u3 = u.reshape(groups, SUBLANES, d_rnn)
    sub = lax.broadcasted_iota(jnp.int32, a3.shape, 1)
    d = 1
    while d < SUBLANES:
        keep = sub >= d
        a_sh = jnp.where(keep, pltpu.roll(a3, d, 1), 1.0)
        u_sh = jnp.where(keep, pltpu.roll(u3, d, 1), 0.0)
        u3 = a3 * u_sh + u3
        a3 = a3 * a_sh
        d *= 2
    carry = h_ref[...]
    hs = []
    for g in range(groups):
        hg = u3[g] + a3[g] * carry
        hs.append(hg)
        carry = hg[SUBLANES - 1:SUBLANES, :]
    h_ref[...] = carry
    o_ref[...] = (jnp.concatenate(hs, axis=0) * jax.nn.gelu(yr)).astype(BF16)


def _rglru(x2d, g1, wxy, conv_w, conv_b, wa_bd, b_a, wx_bd, b_x, rg_lambda, batch, seq, ts):
    t, d = x2d.shape
    d_rnn = conv_w.shape[1]
    ns = seq // ts
    ng = d_rnn // MXU_TILE
    const2 = lambda b, s: (0, 0)
    const3 = lambda b, s: (0, 0, 0)
    return pl.pallas_call(
        functools.partial(_rglru_kernel, ts=ts, d_rnn=d_rnn),
        out_shape=jax.ShapeDtypeStruct((t, d_rnn), BF16),
        grid=(batch, ns),
        in_specs=[
            pl.BlockSpec((ts, d), lambda b, s: (b * ns + s, 0)),
            pl.BlockSpec((1, d), const2),
            pl.BlockSpec((d, 2 * d_rnn), const2),
            pl.BlockSpec((CONV_WIDTH, d_rnn), const2),
            pl.BlockSpec((1, d_rnn), const2),
            pl.BlockSpec((ng, MXU_TILE, MXU_TILE), const3),
            pl.BlockSpec((1, d_rnn), const2),
            pl.BlockSpec((ng, MXU_TILE, MXU_TILE), const3),
            pl.BlockSpec((1, d_rnn), const2),
            pl.BlockSpec((1, d_rnn), const2),
        ],
        out_specs=pl.BlockSpec((ts, d_rnn), lambda b, s: (b * ns + s, 0)),
        scratch_shapes=[pltpu.VMEM((ts + SUBLANES, d_rnn), F32), pltpu.VMEM((1, d_rnn), F32)],
        compiler_params=_params("parallel", "arbitrary"),
        name="rglru",
    )(x2d, g1, wxy, conv_w, conv_b, wa_bd, b_a, wx_bd, b_x, rg_lambda)


def _merge_kernel(x_ref, attn_ref, rnn_ref, g1_ref, wg_ref, bg_ref, pa_ref, pr_ref, wo_ref, g2_ref,
                  x2_ref, xn_ref, *, d):
    x = x_ref[...]
    h = _rms(x, g1_ref[...]).astype(BF16)
    gates = jax.nn.sigmoid(_dot(h, wg_ref[...]) + bg_ref[...])
    merged = gates[:, :d] * _dot(attn_ref[...], pa_ref[...]) + gates[:, d:] * _dot(rnn_ref[...], pr_ref[...])
    x2 = x + _dot(merged.astype(BF16), wo_ref[...])
    x2_ref[...] = x2
    xn_ref[...] = _rms(x2, g2_ref[...]).T.astype(BF16)


def _merge(x2d, attn, rnn, g1, wg, bg, pa, pr, wo, g2, tm):
    t, d = x2d.shape
    d_rnn = rnn.shape[1]
    const = lambda i: (0, 0)
    row = lambda i: (i, 0)
    return pl.pallas_call(
        functools.partial(_merge_kernel, d=d),
        out_shape=(jax.ShapeDtypeStruct((t, d), F32), jax.ShapeDtypeStruct((d, t), BF16)),
        grid=(t // tm,),
        in_specs=[
            pl.BlockSpec((tm, d), row), pl.BlockSpec((tm, d), row), pl.BlockSpec((tm, d_rnn), row),
            pl.BlockSpec((1, d), const), pl.BlockSpec((d, 2 * d), const), pl.BlockSpec((1, 2 * d), const),
            pl.BlockSpec((d, d), const), pl.BlockSpec((d_rnn, d), const), pl.BlockSpec((d, d), const),
            pl.BlockSpec((1, d), const),
        ],
        out_specs=(pl.BlockSpec((tm, d), row), pl.BlockSpec((d, tm), lambda i: (0, i))),
        compiler_params=_params("parallel"),
        name="merge",
    )(x2d, attn, rnn, g1, wg, bg, pa, pr, wo, g2)


def _top_values(s, k, with_rank=False):
    vals = []
    cur = s
    rank = jnp.full(s.shape, float(k), F32) if with_rank else None
    for r in range(k):
        m = jnp.max(cur, axis=0, keepdims=True)
        vals.append(m)
        eq = cur == m
        if with_rank:
            rank = jnp.where(eq, float(r), rank)
        if r + 1 < k:
            cur = jnp.where(eq, -jnp.inf, cur)
    return (vals, rank) if with_rank else vals


def _route_kernel(xn_ref, wqt_ref, keys_ref, n_ref, e1_ref, rank2_ref, e2_ref, *, tm):
    qt = _dot(wqt_ref[...], xn_ref[...]).astype(BF16)
    nk = PEER_N_KEYS
    for h in range(PEER_HEADS):
        s1 = _dot(keys_ref[2 * h], qt[(2 * h) * nk:(2 * h + 1) * nk, :])
        s2 = _dot(keys_ref[2 * h + 1], qt[(2 * h + 1) * nk:(2 * h + 2) * nk, :])
        v1 = _top_values(s1, PEER_TOPK)
        v2, rank2 = _top_values(s2, PEER_TOPK, with_rank=True)
        cands = [v1[i] + v2[j] for i in range(PEER_TOPK) for j in range(PEER_TOPK)
                 if (i + 1) * (j + 1) <= PEER_TOPK]
        pad = (-len(cands)) % SUBLANES
        cands += [jnp.full((1, tm), -jnp.inf, F32)] * pad
        best = _top_values(jnp.concatenate(cands, axis=0), PEER_TOPK)
        tau = best[PEER_TOPK - 1]
        z = sum(jnp.exp(b - best[0]) for b in best)
        n = jnp.zeros(s1.shape, F32)
        for j in range(PEER_TOPK):
            n = n + jnp.where(s1 + v2[j] >= tau, 1.0, 0.0)
        n_ref[h] = n
        e1_ref[h] = 0.5 * jnp.exp(s1 - v1[0]) / z
        rank2_ref[h] = rank2.astype(BF16)
        e2_ref[h] = jnp.exp(s2 - v2[0]).astype(BF16)


def _peer_route(xn, wqt, keys, tm):
    d, t = xn.shape
    nk = PEER_N_KEYS
    words = jax.ShapeDtypeStruct((PEER_HEADS, nk, t), F32)
    halves = jax.ShapeDtypeStruct((PEER_HEADS, nk, t), BF16)
    ospec = pl.BlockSpec((PEER_HEADS, nk, tm), lambda i: (0, 0, i))
    return pl.pallas_call(
        functools.partial(_route_kernel, tm=tm),
        out_shape=(words, words, halves, halves),
        grid=(t // tm,),
        in_specs=[
            pl.BlockSpec((d, tm), lambda i: (0, i)),
            pl.BlockSpec(wqt.shape, lambda i: (0, 0)),
            pl.BlockSpec(keys.shape, lambda i: (0, 0, 0)),
        ],
        out_specs=(ospec, ospec, ospec, ospec),
        compiler_params=_params("parallel"),
        name="peer_route",
    )(xn, wqt, keys)


GELU_C0 = math.sqrt(2.0 / math.pi)
GELU_C1 = GELU_C0 * 0.044715


def _peer_dense_kernel(xn_ref, x2_ref, u_ref, vt_ref, n_ref, e1_ref, rank2_ref, e2_ref, o_ref,
                       acc_ref, s_ref, w_ref, *, na):
    j = pl.program_id(1)
    nk = PEER_N_KEYS
    tm = s_ref.shape[1]

    @pl.when(j == 0)
    def _():
        acc_ref[...] = jnp.zeros(acc_ref.shape, F32)

    s_ref[...] = _dot(u_ref[...], xn_ref[...])

    for a in range(na):
        rows = slice(a * nk, (a + 1) * nk)
        coef = None
        for h in range(PEER_HEADS):
            n_b = jnp.broadcast_to(n_ref[h, a:a + 1, :].astype(BF16), (nk, tm))
            e_b = jnp.broadcast_to(e1_ref[h, a:a + 1, :].astype(BF16), (nk, tm))
            term = e_b * jnp.where(rank2_ref[h] < n_b, e2_ref[h], jnp.zeros((), BF16))
            coef = term if coef is None else coef + term
        x = s_ref[rows, :]
        th = jnp.tanh(x * (GELU_C0 + GELU_C1 * (x * x)))
        w_ref[rows, :] = coef * (x + x * th).astype(BF16)
    acc_ref[...] += _dot(vt_ref[...], w_ref[...])

    @pl.when(j == pl.num_programs(1) - 1)
    def _():
        o_ref[...] = x2_ref[...] + acc_ref[...].T


def _peer_dense(xn, x2, u_bf, vt_bf, n_w, e1_w, rank2, e2, tm, na):
    d, t = xn.shape
    n_exp = u_bf.shape[0]
    nk = PEER_N_KEYS
    te = na * nk
    return pl.pallas_call(
        functools.partial(_peer_dense_kernel, na=na),
        out_shape=jax.ShapeDtypeStruct((t, d), F32),
        grid=(t // tm, n_exp // te),
        in_specs=[
            pl.BlockSpec((d, tm), lambda i, j: (0, i)),
            pl.BlockSpec((tm, d), lambda i, j: (i, 0)),
            pl.BlockSpec((te, d), lambda i, j: (j, 0)),
            pl.BlockSpec((d, te), lambda i, j: (0, j)),
            pl.BlockSpec((PEER_HEADS, na, tm), lambda i, j: (0, j, i)),
            pl.BlockSpec((PEER_HEADS, na, tm), lambda i, j: (0, j, i)),
            pl.BlockSpec((PEER_HEADS, nk, tm), lambda i, j: (0, 0, i)),
            pl.BlockSpec((PEER_HEADS, nk, tm), lambda i, j: (0, 0, i)),
        ],
        out_specs=pl.BlockSpec((tm, d), lambda i, j: (i, 0)),
        scratch_shapes=[pltpu.VMEM((d, tm), F32), pltpu.VMEM((te, tm), F32), pltpu.VMEM((te, tm), BF16)],
        compiler_params=_params("parallel", "arbitrary"),
        name="peer_dense",
    )(xn, x2, u_bf, vt_bf, n_w, e1_w, rank2, e2)


def _block_diag_tiles(w):
    nb, r, _ = w.shape
    per = MXU_TILE // r
    w = w.reshape(nb // per, per, r, r)
    eye = jnp.eye(per, dtype=w.dtype)
    return jnp.einsum("gpij,pq->gpiqj", w, eye).reshape(nb // per, MXU_TILE, MXU_TILE)


def _layer(x, lam_init, norm1_g, w_in, b_gate, q_norm_g, k_norm_g, lambda_q1, lambda_k1, lambda_q2,
           lambda_k2, subln_g, conv_w, conv_b, w_rg_a, b_rg_a, w_rg_x, b_rg_x, rg_lambda, w_br_attn,
           w_br_rnn, w_out, norm2_g, w_peer_q, peer_sub_keys, peer_u, peer_v):
    batch, seq, d = x.shape
    t = batch * seq
    d_rnn = conv_w.shape[1]
    x2d = x.reshape(t, d)
    row = lambda v: v.reshape(1, -1)

    w_in_bf = w_in.astype(BF16)
    n_grp = d // HEAD_DIM
    scale = HEAD_DIM ** -0.5
    gn = jnp.stack([jnp.tile(q_norm_g, n_grp) * (scale * LOG2_E), jnp.tile(k_norm_g, n_grp),
                    jnp.ones((d,), F32)]).reshape(3, 1, d)
    gsum = jnp.kron(jnp.eye(MXU_TILE // HEAD_DIM, dtype=F32),
                    jnp.full((HEAD_DIM, HEAD_DIM), 1.0 / HEAD_DIM, F32)).astype(BF16)
    wxy = w_in_bf[:, 3 * d:3 * d + 2 * d_rnn]
    wg = w_in_bf[:, 3 * d + 2 * d_rnn:]
    wa_bd = _block_diag_tiles(w_rg_a).astype(BF16)
    wx_bd = _block_diag_tiles(w_rg_x).astype(BF16)
    wqt = w_peer_q.T.astype(BF16)
    keys = peer_sub_keys.reshape(PEER_HEADS * 2, PEER_N_KEYS, -1).astype(BF16)
    u_bf = peer_u.astype(BF16)
    vt_bf = peer_v.T.astype(BF16)

    g1 = row(norm1_g)
    qkv = _qkv_proj(x2d, g1, w_in_bf, gn, gsum, tm=512)
    attn = _diff_attention(qkv, row(lambda_q1), row(lambda_k1), row(lambda_q2), row(lambda_k2),
                           row(subln_g), batch, seq, lam_init, tq=512)
    rnn = _rglru(x2d, g1, wxy, conv_w, row(conv_b), wa_bd, row(b_rg_a), wx_bd, row(b_rg_x),
                 row(rg_lambda), batch, seq, ts=256)
    x2, xn = _merge(x2d, attn, rnn, g1, wg, row(b_gate), w_br_attn.astype(BF16), w_br_rnn.astype(BF16),
                    w_out.astype(BF16), row(norm2_g), tm=256)
    n_w, e1_w, rank2, e2 = _peer_route(xn, wqt, keys, tm=256)
    out = _peer_dense(xn, x2, u_bf, vt_bf, n_w, e1_w, rank2, e2, tm=512, na=16)
    return out.reshape(batch, seq, d)


def kernel(x, norm1_g, w_in, b_gate, q_norm_g, k_norm_g, lambda_q1, lambda_k1, lambda_q2, lambda_k2,
           subln_g, conv_w, conv_b, w_rg_a, b_rg_a, w_rg_x, b_rg_x, rg_lambda, w_br_attn, w_br_rnn,
           w_out, norm2_g, w_peer_q, peer_sub_keys, peer_u, peer_v):
    depth = norm1_g.shape[0]
    for layer in range(depth):
        lam_init = 0.8 - 0.6 * math.exp(-0.3 * layer)
        x = _layer(x, lam_init, norm1_g[layer], w_in[layer], b_gate[layer], q_norm_g[layer],
                   k_norm_g[layer], lambda_q1[layer], lambda_k1[layer], lambda_q2[layer],
                   lambda_k2[layer], subln_g[layer], conv_w[layer], conv_b[layer], w_rg_a[layer],
                   b_rg_a[layer], w_rg_x[layer], b_rg_x[layer], rg_lambda[layer], w_br_attn[layer],
                   w_br_rnn[layer], w_out[layer], norm2_g[layer], w_peer_q[layer], peer_sub_keys[layer],
                   peer_u[layer], peer_v[layer])
    return x
```

```python
import functools
import math

import jax
import jax.numpy as jnp
from jax import lax
from jax.experimental import pallas as pl
from jax.experimental.pallas import tpu as pltpu

F32 = jnp.float32
BF16 = jnp.bfloat16

EPS = 1e-6
NEG_INF = -1e30
LOG2_E = math.log2(math.e)
CHUNK = 64
N_HEADS = 8
HEAD_DIM = 64
V_HEAD_DIM = 128
RNN_BLOCK = 64
CONV_WIDTH = 4
RG_C = 8.0
PEER_HEADS = 8
PEER_N_KEYS = 128
PEER_TOPK = 16

MXU_TILE = 256
SUBLANES = 8
VMEM_LIMIT = 56 * 1024 * 1024


def _params(*sem):
    return pltpu.CompilerParams(dimension_semantics=sem, vmem_limit_bytes=VMEM_LIMIT)


def _rms(x, g):
    ms = jnp.mean(x * x, axis=-1, keepdims=True)
    return x * lax.rsqrt(ms + EPS) * g


def _dot(a, b):
    return jnp.dot(a, b, preferred_element_type=F32)


def _dot_nt(a, b):
    return lax.dot_general(a, b, (((1,), (1,)), ((), ())), preferred_element_type=F32)


def _qkv_kernel(x_ref, g1_ref, w_ref, gn_ref, gsum_ref, o_ref, h_ref):
    j = pl.program_id(1)

    @pl.when(j == 0)
    def _():
        h_ref[...] = _rms(x_ref[...], g1_ref[...]).astype(BF16)

    y = _dot(h_ref[...], w_ref[...])

    @pl.when(j < 2)
    def _():
        y2 = y * y
        hi = y2.astype(BF16)
        lo = (y2 - hi.astype(F32)).astype(BF16)
        g = gsum_ref[...]
        ms = jnp.concatenate(
            [_dot(hi[:, c:c + MXU_TILE], g) + _dot(lo[:, c:c + MXU_TILE], g)
             for c in range(0, y.shape[1], MXU_TILE)], axis=-1)
        o_ref[...] = (y * lax.rsqrt(ms + EPS) * gn_ref[0]).astype(BF16)

    @pl.when(j == 2)
    def _():
        o_ref[...] = y.astype(BF16)


def _qkv_proj(x2d, g1, w_in_bf, gn, gsum, tm):
    t, d = x2d.shape
    return pl.pallas_call(
        _qkv_kernel,
        out_shape=jax.ShapeDtypeStruct((t, 3 * d), BF16),
        grid=(t // tm, 3),
        in_specs=[
            pl.BlockSpec((tm, d), lambda i, j: (i, 0)),
            pl.BlockSpec((1, d), lambda i, j: (0, 0)),
            pl.BlockSpec((d, d), lambda i, j: (0, j)),
            pl.BlockSpec((1, 1, d), lambda i, j: (j, 0, 0)),
            pl.BlockSpec((MXU_TILE, MXU_TILE), lambda i, j: (0, 0)),
        ],
        out_specs=pl.BlockSpec((tm, d), lambda i, j: (i, j)),
        scratch_shapes=[pltpu.VMEM((tm, d), BF16)],
        compiler_params=_params("parallel", "arbitrary"),
        name="qkv_proj",
    )(x2d, g1, w_in_bf, gn, gsum)


def _attn_kernel(q_ref, k_ref, v_ref, lq1_ref, lk1_ref, lq2_ref, lk2_ref, sg_ref, o_ref,
                 s_buf, mrun_ref, m_ref, acc_ref, vx_ref, bias_ref, *, tq, lam_init):
    i = pl.program_id(2)
    hd = V_HEAD_DIM

    @pl.when((pl.program_id(0) == 0) & (pl.program_id(1) == 0) & (i == 0))
    def _():
        r = lax.broadcasted_iota(jnp.int32, bias_ref.shape, 0)
        c = lax.broadcasted_iota(jnp.int32, bias_ref.shape, 1)
        qpos = jnp.where(r >= tq, r - tq, r)
        bias_ref[...] = jnp.where((c // CHUNK) <= (qpos // CHUNK), 0.0, NEG_INF)

    @pl.when(i == 0)
    def _():
        vx_ref[:, :hd] = v_ref[...]
        vx_ref[:, hd:] = jnp.ones((vx_ref.shape[0], hd), BF16)

    q = q_ref[...]
    lane = lax.broadcasted_iota(jnp.int32, q.shape, 1)
    zero = jnp.zeros_like(q)
    qs = jnp.concatenate([jnp.where(lane < HEAD_DIM, q, zero), jnp.where(lane >= HEAD_DIM, q, zero)], axis=0)

    mrun_ref[...] = jnp.full(mrun_ref.shape, NEG_INF, F32)

    def scores(j, masked):
        start = pl.multiple_of(j * tq, tq)
        s = _dot_nt(qs, k_ref[pl.ds(start, tq), :])
        if masked:
            s = s + bias_ref[...]
        s_buf[j] = s
        m = mrun_ref[...]
        for c0 in range(0, tq, hd):
            m = jnp.maximum(m, s[:, c0:c0 + hd])
        mrun_ref[...] = m

    def full_scores(p, carry):
        scores(2 * p, False)
        scores(2 * p + 1, False)
        return carry

    lax.fori_loop(0, i // 2, full_scores, 0)

    @pl.when(i % 2 == 1)
    def _():
        scores(i - 1, False)

    scores(i, True)
    m_ref[...] = jnp.broadcast_to(jnp.max(mrun_ref[...], axis=-1, keepdims=True), m_ref.shape)

    acc_ref[...] = jnp.zeros(acc_ref.shape, F32)

    def weighted(j):
        start = pl.multiple_of(j * tq, tq)
        s = s_buf[j]
        m = m_ref[...]
        p = jnp.concatenate([jnp.exp2(s[:, c0:c0 + hd] - m) for c0 in range(0, tq, hd)], axis=-1)
        return _dot(p.astype(BF16), vx_ref[pl.ds(start, tq), :])

    def pair_weighted(p, carry):
        acc_ref[...] += weighted(2 * p) + weighted(2 * p + 1)
        return carry

    lax.fori_loop(0, (i + 1) // 2, pair_weighted, 0)

    @pl.when(i % 2 == 0)
    def _():
        acc_ref[...] += weighted(i)

    lam = (jnp.exp(jnp.sum(lq1_ref[...] * lk1_ref[...], axis=-1, keepdims=True))
           - jnp.exp(jnp.sum(lq2_ref[...] * lk2_ref[...], axis=-1, keepdims=True)) + lam_init)
    o = acc_ref[:, :hd] / acc_ref[:, hd:]
    o = o[:tq] - lam * o[tq:]
    o_ref[...] = (_rms(o, sg_ref[...]) * (1.0 - lam_init)).astype(BF16)


def _diff_attention(qkv, lq1, lk1, lq2, lk2, subln_g, batch, seq, lam_init, tq):
    t = qkv.shape[0]
    nq = seq // tq
    hd = V_HEAD_DIM
    small = pl.BlockSpec((1, HEAD_DIM), lambda b, h, i: (0, 0))
    return pl.pallas_call(
        functools.partial(_attn_kernel, tq=tq, lam_init=lam_init),
        out_shape=jax.ShapeDtypeStruct((t, N_HEADS * hd), BF16),
        grid=(batch, N_HEADS, nq),
        in_specs=[
            pl.BlockSpec((tq, hd), lambda b, h, i: (b * nq + i, h)),
            pl.BlockSpec((seq, hd), lambda b, h, i: (b, N_HEADS + h)),
            pl.BlockSpec((seq, hd), lambda b, h, i: (b, 2 * N_HEADS + h)),
            small, small, small, small,
            pl.BlockSpec((1, hd), lambda b, h, i: (0, 0)),
        ],
        out_specs=pl.BlockSpec((tq, hd), lambda b, h, i: (b * nq + i, h)),
        scratch_shapes=[pltpu.VMEM((nq, 2 * tq, tq), F32), pltpu.VMEM((2 * tq, hd), F32),
                        pltpu.VMEM((2 * tq, hd), F32), pltpu.VMEM((2 * tq, 2 * hd), F32),
                        pltpu.VMEM((seq, 2 * hd), BF16), pltpu.VMEM((2 * tq, tq), F32)],
        compiler_params=_params("arbitrary", "arbitrary", "arbitrary"),
        name="diff_attn",
    )(qkv, qkv, qkv, lq1, lk1, lq2, lk2, subln_g)


def _rglru_kernel(x_ref, g1_ref, wxy_ref, cw_ref, cb_ref, wa_ref, ba_ref, wx_ref, bx_ref, lam_ref, o_ref,
                  xbuf_ref, h_ref, *, ts, d_rnn):
    s_idx = pl.program_id(1)

    @pl.when(s_idx == 0)
    def _():
        xbuf_ref[0:SUBLANES, :] = jnp.zeros((SUBLANES, d_rnn), F32)
        h_ref[...] = jnp.zeros(h_ref.shape, F32)

    h_in = _rms(x_ref[...], g1_ref[...]).astype(BF16)
    xy = _dot(h_in, wxy_ref[...])
    xr = xy[:, :d_rnn]
    yr = xy[:, d_rnn:]

    xbuf_ref[SUBLANES:SUBLANES + ts, :] = xr
    conv = cb_ref[...] + jnp.zeros((ts, d_rnn), F32)
    for tap in range(CONV_WIDTH):
        off = SUBLANES - (CONV_WIDTH - 1) + tap
        conv = conv + xbuf_ref[off:off + ts, :] * cw_ref[tap:tap + 1, :]
    xbuf_ref[0:SUBLANES, :] = xbuf_ref[ts:ts + SUBLANES, :]

    cbf = conv.astype(BF16)
    ga, gi = [], []
    for g in range(d_rnn // MXU_TILE):
        sl = slice(g * MXU_TILE, (g + 1) * MXU_TILE)
        ga.append(_dot(cbf[:, sl], wa_ref[g]))
        gi.append(_dot(cbf[:, sl], wx_ref[g]))
    gate_r = jax.nn.sigmoid(jnp.concatenate(ga, axis=-1) + ba_ref[...])
    gate_i = jax.nn.sigmoid(jnp.concatenate(gi, axis=-1) + bx_ref[...])

    neg_lam = -lam_ref[...]
    softplus = jnp.maximum(neg_lam, 0.0) + jnp.log1p(jnp.exp(-jnp.abs(neg_lam)))
    log_a = -RG_C * gate_r * softplus
    a = jnp.exp(log_a)
    th = jnp.tanh(log_a)
    mult = jnp.sqrt(-2.0 * th / (1.0 - th))
    u = mult * (gate_i * conv)

    groups = ts // SUBLANES
    a3 = a.reshape(groups, SUBLANES, d_rnn)
    u3 = u.reshape(groups, SUBLANES, d_rnn)
    sub = lax.broadcasted_iota(jnp.int32, a3.shape, 1)
    d = 1
    while d < SUBLANES:
        keep = sub >= d
        a_sh = jnp.where(keep, pltpu.roll(a3, d, 1), 1.0)
        u_sh = jnp.where(keep, pltpu.roll(u3, d, 1), 0.0)
        u3 = a3 * u_sh + u3
        a3 = a3 * a_sh
        d *= 2
    carry = h_ref[...]
    hs = []
    for g in range(groups):
        hg = u3[g] + a3[g] * carry
        hs.append(hg)
        carry = hg[SUBLANES - 1:SUBLANES, :]
    h_ref[...] = carry
    o_ref[...] = (jnp.concatenate(hs, axis=0) * jax.nn.gelu(yr)).astype(BF16)


def _rglru(x2d, g1, wxy, conv_w, conv_b, wa_bd, b_a, wx_bd, b_x, rg_lambda, batch, seq, ts):
    t, d = x2d.shape
    d_rnn = conv_w.shape[1]
    ns = seq // ts
    ng = d_rnn // MXU_TILE
    const2 = lambda b, s: (0, 0)
    const3 = lambda b, s: (0, 0, 0)
    return pl.pallas_call(
        functools.partial(_rglru_kernel, ts=ts, d_rnn=d_rnn),
        out_shape=jax.ShapeDtypeStruct((t, d_rnn), BF16),
        grid=(batch, ns),
        in_specs=[
            pl.BlockSpec((ts, d), lambda b, s: (b * ns + s, 0)),
            pl.BlockSpec((1, d), const2),
            pl.BlockSpec((d, 2 * d_rnn), const2),
            pl.BlockSpec((CONV_WIDTH, d_rnn), const2),
            pl.BlockSpec((1, d_rnn), const2),
            pl.BlockSpec((ng, MXU_TILE, MXU_TILE), const3),
            pl.BlockSpec((1, d_rnn), const2),
            pl.BlockSpec((ng, MXU_TILE, MXU_TILE), const3),
            pl.BlockSpec((1, d_rnn), const2),
            pl.BlockSpec((1, d_rnn), const2),
        ],
        out_specs=pl.BlockSpec((ts, d_rnn), lambda b, s: (b * ns + s, 0)),
        scratch_shapes=[pltpu.VMEM((ts + SUBLANES, d_rnn), F32), pltpu.VMEM((1, d_rnn), F32)],
        compiler_params=_params("parallel", "arbitrary"),
        name="rglru",
    )(x2d, g1, wxy, conv_w, conv_b, wa_bd, b_a, wx_bd, b_x, rg_lambda)


def _merge_kernel(x_ref, attn_ref, rnn_ref, g1_ref, wg_ref, bg_ref, pa_ref, pr_ref, wo_ref, g2_ref,
                  x2_ref, xn_ref, *, d):
    x = x_ref[...]
    h = _rms(x, g1_ref[...]).astype(BF16)
    gates = jax.nn.sigmoid(_dot(h, wg_ref[...]) + bg_ref[...])
    merged = gates[:, :d] * _dot(attn_ref[...], pa_ref[...]) + gates[:, d:] * _dot(rnn_ref[...], pr_ref[...])
    x2 = x + _dot(merged.astype(BF16), wo_ref[...])
    x2_ref[...] = x2
    xn_ref[...] = _rms(x2, g2_ref[...]).T.astype(BF16)


def _merge(x2d, attn, rnn, g1, wg, bg, pa, pr, wo, g2, tm):
    t, d = x2d.shape
    d_rnn = rnn.shape[1]
    const = lambda i: (0, 0)
    row = lambda i: (i, 0)
    return pl.pallas_call(
        functools.partial(_merge_kernel, d=d),
        out_shape=(jax.ShapeDtypeStruct((t, d), F32), jax.ShapeDtypeStruct((d, t), BF16)),
        grid=(t // tm,),
        in_specs=[
            pl.BlockSpec((tm, d), row), pl.BlockSpec((tm, d), row), pl.BlockSpec((tm, d_rnn), row),
            pl.BlockSpec((1, d), const), pl.BlockSpec((d, 2 * d), const), pl.BlockSpec((1, 2 * d), const),
            pl.BlockSpec((d, d), const), pl.BlockSpec((d_rnn, d), const), pl.BlockSpec((d, d), const),
            pl.BlockSpec((1, d), const),
        ],
        out_specs=(pl.BlockSpec((tm, d), row), pl.BlockSpec((d, tm), lambda i: (0, i))),
        compiler_params=_params("parallel"),
        name="merge",
    )(x2d, attn, rnn, g1, wg, bg, pa, pr, wo, g2)


def _top_values(s, k, with_rank=False):
    vals = []
    cur = s
    rank = jnp.full(s.shape, float(k), F32) if with_rank else None
    for r in range(k):
        m = jnp.max(cur, axis=0, keepdims=True)
        vals.append(m)
        eq = cur == m
        if with_rank:
            rank = jnp.where(eq, float(r), rank)
        if r + 1 < k:
            cur = jnp.where(eq, -jnp.inf, cur)
    return (vals, rank) if with_rank else vals


def _route_kernel(xn_ref, wqt_ref, keys_ref, n_ref, e1_ref, rank2_ref, e2_ref, *, tm):
    qt = _dot(wqt_ref[...], xn_ref[...]).astype(BF16)
    nk = PEER_N_KEYS
    for h in range(PEER_HEADS):
        s1 = _dot(keys_ref[2 * h], qt[(2 * h) * nk:(2 * h + 1) * nk, :])
        s2 = _dot(keys_ref[2 * h + 1], qt[(2 * h + 1) * nk:(2 * h + 2) * nk, :])
        v1 = _top_values(s1, PEER_TOPK)
        v2, rank2 = _top_values(s2, PEER_TOPK, with_rank=True)
        cands = [v1[i] + v2[j] for i in range(PEER_TOPK) for j in range(PEER_TOPK)
                 if (i + 1) * (j + 1) <= PEER_TOPK]
        pad = (-len(cands)) % SUBLANES
        cands += [jnp.full((1, tm), -jnp.inf, F32)] * pad
        best = _top_values(jnp.concatenate(cands, axis=0), PEER_TOPK)
        tau = best[PEER_TOPK - 1]
        z = sum(jnp.exp(b - best[0]) for b in best)
        v1_rows = jnp.concatenate(v1, axis=0)
        m = sum(jnp.where(v1_rows + v2[j] >= tau, 1.0, 0.0) for j in range(PEER_TOPK))
        n = jnp.zeros(s1.shape, F32)
        for i in range(PEER_TOPK):
            n = jnp.where(s1 == v1[i], m[i:i + 1, :], n)
        n_ref[h] = n
        e1_ref[h] = 0.5 * jnp.exp(s1 - v1[0]) / z
        rank2_ref[h] = rank2.astype(BF16)
        e2_ref[h] = jnp.exp(s2 - v2[0]).astype(BF16)


def _peer_route(xn, wqt, keys, tm):
    d, t = xn.shape
    nk = PEER_N_KEYS
    words = jax.ShapeDtypeStruct((PEER_HEADS, nk, t), F32)
    halves = jax.ShapeDtypeStruct((PEER_HEADS, nk, t), BF16)
    ospec = pl.BlockSpec((PEER_HEADS, nk, tm), lambda i: (0, 0, i))
    return pl.pallas_call(
        functools.partial(_route_kernel, tm=tm),
        out_shape=(words, words, halves, halves),
        grid=(t // tm,),
        in_specs=[
            pl.BlockSpec((d, tm), lambda i: (0, i)),
            pl.BlockSpec(wqt.shape, lambda i: (0, 0)),
            pl.BlockSpec(keys.shape, lambda i: (0, 0, 0)),
        ],
        out_specs=(ospec, ospec, ospec, ospec),
        compiler_params=_params("parallel"),
        name="peer_route",
    )(xn, wqt, keys)


GELU_C0 = math.sqrt(2.0 / math.pi)
GELU_C1 = GELU_C0 * 0.044715


def _peer_dense_kernel(xn_ref, x2_ref, u_ref, vt_ref, n_ref, e1_ref, rank2_ref, e2_ref, o_ref,
                       acc_ref, s_ref, w_ref, *, na, tiles_per_token_tile):
    n = pl.program_id(0)
    nk = PEER_N_KEYS
    tm = s_ref.shape[1]
    j = jnp.maximum(n - 1, 0) % tiles_per_token_tile

    @pl.when(n == 0)
    def _():
        s_ref[...] = _dot(u_ref[...], xn_ref[...])

    @pl.when((n > 0) & (j == 0))
    def _():
        acc_ref[...] = jnp.zeros(acc_ref.shape, F32)

    @pl.when(n > 0)
    def _():
        for a in range(na):
            rows = slice(a * nk, (a + 1) * nk)
            coef = None
            for h in range(PEER_HEADS):
                n_b = jnp.broadcast_to(n_ref[h, a:a + 1, :].astype(BF16), (nk, tm))
                e_b = jnp.broadcast_to(e1_ref[h, a:a + 1, :].astype(BF16), (nk, tm))
                term = e_b * jnp.where(rank2_ref[h] < n_b, e2_ref[h], jnp.zeros((), BF16))
                coef = term if coef is None else coef + term
            x = s_ref[rows, :]
            th = jnp.tanh(x * (GELU_C0 + GELU_C1 * (x * x)))
            w_ref[rows, :] = coef * (x + x * th).astype(BF16)
        acc_ref[...] += _dot(vt_ref[...], w_ref[...])
        s_ref[...] = _dot(u_ref[...], xn_ref[...])

    @pl.when((n > 0) & (j == tiles_per_token_tile - 1))
    def _():
        o_ref[...] = x2_ref[...] + acc_ref[...].T


def _peer_dense(xn, x2, u_bf, vt_bf, n_w, e1_w, rank2, e2, tm, na):
    d, t = xn.shape
    n_exp = u_bf.shape[0]
    nk = PEER_N_KEYS
    te = na * nk
    nj = n_exp // te
    n_tiles = (t // tm) * nj

    def ahead(n):
        m = jnp.minimum(n, n_tiles - 1)
        return m // nj, m % nj

    def now(n):
        m = jnp.maximum(n - 1, 0)
        return m // nj, m % nj

    return pl.pallas_call(
        functools.partial(_peer_dense_kernel, na=na, tiles_per_token_tile=nj),
        out_shape=jax.ShapeDtypeStruct((t, d), F32),
        grid=(n_tiles + 1,),
        in_specs=[
            pl.BlockSpec((d, tm), lambda n: (0, ahead(n)[0])),
            pl.BlockSpec((tm, d), lambda n: (now(n)[0], 0)),
            pl.BlockSpec((te, d), lambda n: (ahead(n)[1], 0)),
            pl.BlockSpec((d, te), lambda n: (0, now(n)[1])),
            pl.BlockSpec((PEER_HEADS, na, tm), lambda n: (0, now(n)[1], now(n)[0])),
            pl.BlockSpec((PEER_HEADS, na, tm), lambda n: (0, now(n)[1], now(n)[0])),
            pl.BlockSpec((PEER_HEADS, nk, tm), lambda n: (0, 0, now(n)[0])),
            pl.BlockSpec((PEER_HEADS, nk, tm), lambda n: (0, 0, now(n)[0])),
        ],
        out_specs=pl.BlockSpec((tm, d), lambda n: (now(n)[0], 0)),
        scratch_shapes=[pltpu.VMEM((d, tm), F32), pltpu.VMEM((te, tm), F32), pltpu.VMEM((te, tm), BF16)],
        compiler_params=_params("arbitrary"),
        name="peer_dense",
    )(xn, x2, u_bf, vt_bf, n_w, e1_w, rank2, e2)


def _block_diag_tiles(w):
    nb, r, _ = w.shape
    per = MXU_TILE // r
    w = w.reshape(nb // per, per, r, r)
    eye = jnp.eye(per, dtype=w.dtype)
    return jnp.einsum("gpij,pq->gpiqj", w, eye).reshape(nb // per, MXU_TILE, MXU_TILE)


def _layer(x, lam_init, norm1_g, w_in, b_gate, q_norm_g, k_norm_g, lambda_q1, lambda_k1, lambda_q2,
           lambda_k2, subln_g, conv_w, conv_b, w_rg_a, b_rg_a, w_rg_x, b_rg_x, rg_lambda, w_br_attn,
           w_br_rnn, w_out, norm2_g, w_peer_q, peer_sub_keys, peer_u, peer_v):
    batch, seq, d = x.shape
    t = batch * seq
    d_rnn = conv_w.shape[1]
    x2d = x.reshape(t, d)
    row = lambda v: v.reshape(1, -1)

    w_in_bf = w_in.astype(BF16)
    n_grp = d // HEAD_DIM
    scale = HEAD_DIM ** -0.5
    gn = jnp.stack([jnp.tile(q_norm_g, n_grp) * (scale * LOG2_E), jnp.tile(k_norm_g, n_grp),
                    jnp.ones((d,), F32)]).reshape(3, 1, d)
    gsum = jnp.kron(jnp.eye(MXU_TILE // HEAD_DIM, dtype=F32),
                    jnp.full((HEAD_DIM, HEAD_DIM), 1.0 / HEAD_DIM, F32)).astype(BF16)
    wxy = w_in_bf[:, 3 * d:3 * d + 2 * d_rnn]
    wg = w_in_bf[:, 3 * d + 2 * d_rnn:]
    wa_bd = _block_diag_tiles(w_rg_a).astype(BF16)
    wx_bd = _block_diag_tiles(w_rg_x).astype(BF16)
    wqt = w_peer_q.T.astype(BF16)
    keys = peer_sub_keys.reshape(PEER_HEADS * 2, PEER_N_KEYS, -1).astype(BF16)
    u_bf = peer_u.astype(BF16)
    vt_bf = peer_v.T.astype(BF16)

    g1 = row(norm1_g)
    qkv = _qkv_proj(x2d, g1, w_in_bf, gn, gsum, tm=512)
    attn = _diff_attention(qkv, row(lambda_q1), row(lambda_k1), row(lambda_q2), row(lambda_k2),
                           row(subln_g), batch, seq, lam_init, tq=512)
    rnn = _rglru(x2d, g1, wxy, conv_w, row(conv_b), wa_bd, row(b_rg_a), wx_bd, row(b_rg_x),
                 row(rg_lambda), batch, seq, ts=256)
    x2, xn = _merge(x2d, attn, rnn, g1, wg, row(b_gate), w_br_attn.astype(BF16), w_br_rnn.astype(BF16),
                    w_out.astype(BF16), row(norm2_g), tm=256)
    n_w, e1_w, rank2, e2 = _peer_route(xn, wqt, keys, tm=256)
    out = _peer_dense(xn, x2, u_bf, vt_bf, n_w, e1_w, rank2, e2, tm=512, na=16)
    return out.reshape(batch, seq, d)


def kernel(x, norm1_g, w_in, b_gate, q_norm_g, k_norm_g, lambda_q1, lambda_k1, lambda_q2, lambda_k2,
           subln_g, conv_w, conv_b, w_rg_a, b_rg_a, w_rg_x, b_rg_x, rg_lambda, w_br_attn, w_br_rnn,
           w_out, norm2_g, w_peer_q, peer_sub_keys, peer_u, peer_v):
    depth = norm1_g.shape[0]
    for layer in range(depth):
        lam_init = 0.8 - 0.6 * math.exp(-0.3 * layer)
        x = _layer(x, lam_init, norm1_g[layer], w_in[layer], b_gate[layer], q_norm_g[layer],
                   k_norm_g[layer], lambda_q1[layer], lambda_k1[layer], lambda_q2[layer],
                   lambda_k2[layer], subln_g[layer], conv_w[layer], conv_b[layer], w_rg_a[layer],
                   b_rg_a[layer], w_rg_x[layer], b_rg_x[layer], rg_lambda[layer], w_br_attn[layer],
                   w_br_rnn[layer], w_out[layer], norm2_g[layer], w_peer_q[layer], peer_sub_keys[layer],
                   peer_u[layer], peer_v[layer])
    return x
```

```python
import functools
import math

import jax
import jax.numpy as jnp
from jax import lax
from jax.experimental import pallas as pl
from jax.experimental.pallas import tpu as pltpu

F32 = jnp.float32
BF16 = jnp.bfloat16

EPS = 1e-6
NEG_INF = -1e30
LOG2_E = math.log2(math.e)
CHUNK = 64
N_HEADS = 8
HEAD_DIM = 64
V_HEAD_DIM = 128
RNN_BLOCK = 64
CONV_WIDTH = 4
RG_C = 8.0
PEER_HEADS = 8
PEER_N_KEYS = 128
PEER_TOPK = 16

MXU_TILE = 256
SUBLANES = 8
VMEM_LIMIT = 56 * 1024 * 1024


def _params(*sem):
    return pltpu.CompilerParams(dimension_semantics=sem, vmem_limit_bytes=VMEM_LIMIT)


def _rms(x, g):
    ms = jnp.mean(x * x, axis=-1, keepdims=True)
    return x * lax.rsqrt(ms + EPS) * g


def _dot(a, b):
    return jnp.dot(a, b, preferred_element_type=F32)


def _dot_nt(a, b):
    return lax.dot_general(a, b, (((1,), (1,)), ((), ())), preferred_element_type=F32)


def _qkv_kernel(x_ref, g1_ref, w_ref, gn_ref, gsum_ref, o_ref, h_ref):
    j = pl.program_id(1)

    @pl.when(j == 0)
    def _():
        h_ref[...] = _rms(x_ref[...], g1_ref[...]).astype(BF16)

    y = _dot(h_ref[...], w_ref[...])

    @pl.when(j < 2)
    def _():
        y2 = y * y
        hi = y2.astype(BF16)
        lo = (y2 - hi.astype(F32)).astype(BF16)
        g = gsum_ref[...]
        ms = jnp.concatenate(
            [_dot(hi[:, c:c + MXU_TILE], g) + _dot(lo[:, c:c + MXU_TILE], g)
             for c in range(0, y.shape[1], MXU_TILE)], axis=-1)
        o_ref[...] = (y * lax.rsqrt(ms + EPS) * gn_ref[0]).astype(BF16)

    @pl.when(j == 2)
    def _():
        o_ref[...] = y.astype(BF16)


def _qkv_proj(x2d, g1, w_in_bf, gn, gsum, tm):
    t, d = x2d.shape
    return pl.pallas_call(
        _qkv_kernel,
        out_shape=jax.ShapeDtypeStruct((t, 3 * d), BF16),
        grid=(t // tm, 3),
        in_specs=[
            pl.BlockSpec((tm, d), lambda i, j: (i, 0)),
            pl.BlockSpec((1, d), lambda i, j: (0, 0)),
            pl.BlockSpec((d, d), lambda i, j: (0, j)),
            pl.BlockSpec((1, 1, d), lambda i, j: (j, 0, 0)),
            pl.BlockSpec((MXU_TILE, MXU_TILE), lambda i, j: (0, 0)),
        ],
        out_specs=pl.BlockSpec((tm, d), lambda i, j: (i, j)),
        scratch_shapes=[pltpu.VMEM((tm, d), BF16)],
        compiler_params=_params("parallel", "arbitrary"),
        name="qkv_proj",
    )(x2d, g1, w_in_bf, gn, gsum)


def _attn_kernel(q_ref, k_ref, v_ref, lq1_ref, lk1_ref, lq2_ref, lk2_ref, sg_ref, o_ref,
                 s_buf, mrun_ref, m_ref, acc_ref, vx_ref, bias_ref, *, tq, lam_init):
    i = pl.program_id(2)
    hd = V_HEAD_DIM

    @pl.when((pl.program_id(0) == 0) & (pl.program_id(1) == 0) & (i == 0))
    def _():
        r = lax.broadcasted_iota(jnp.int32, bias_ref.shape, 0)
        c = lax.broadcasted_iota(jnp.int32, bias_ref.shape, 1)
        qpos = jnp.where(r >= tq, r - tq, r)
        bias_ref[...] = jnp.where((c // CHUNK) <= (qpos // CHUNK), 0.0, NEG_INF)

    @pl.when(i == 0)
    def _():
        vx_ref[:, :hd] = v_ref[...]
        vx_ref[:, hd:] = jnp.ones((vx_ref.shape[0], hd), BF16)

    q = q_ref[...]
    lane = lax.broadcasted_iota(jnp.int32, q.shape, 1)
    zero = jnp.zeros_like(q)
    qs = jnp.concatenate([jnp.where(lane < HEAD_DIM, q, zero), jnp.where(lane >= HEAD_DIM, q, zero)], axis=0)

    mrun_ref[...] = jnp.full(mrun_ref.shape, NEG_INF, F32)

    def scores(j, masked):
        start = pl.multiple_of(j * tq, tq)
        s = _dot_nt(qs, k_ref[pl.ds(start, tq), :])
        if masked:
            s = s + bias_ref[...]
        s_buf[j] = s
        m = mrun_ref[...]
        for c0 in range(0, tq, hd):
            m = jnp.maximum(m, s[:, c0:c0 + hd])
        mrun_ref[...] = m

    def full_scores(p, carry):
        scores(2 * p, False)
        scores(2 * p + 1, False)
        return carry

    lax.fori_loop(0, i // 2, full_scores, 0)

    @pl.when(i % 2 == 1)
    def _():
        scores(i - 1, False)

    scores(i, True)
    m_ref[...] = jnp.broadcast_to(jnp.max(mrun_ref[...], axis=-1, keepdims=True), m_ref.shape)

    acc_ref[...] = jnp.zeros(acc_ref.shape, F32)

    def weighted(j):
        start = pl.multiple_of(j * tq, tq)
        s = s_buf[j]
        m = m_ref[...]
        p = jnp.concatenate([jnp.exp2(s[:, c0:c0 + hd] - m) for c0 in range(0, tq, hd)], axis=-1)
        return _dot(p.astype(BF16), vx_ref[pl.ds(start, tq), :])

    def pair_weighted(p, carry):
        acc_ref[...] += weighted(2 * p) + weighted(2 * p + 1)
        return carry

    lax.fori_loop(0, (i + 1) // 2, pair_weighted, 0)

    @pl.when(i % 2 == 0)
    def _():
        acc_ref[...] += weighted(i)

    lam = (jnp.exp(jnp.sum(lq1_ref[...] * lk1_ref[...], axis=-1, keepdims=True))
           - jnp.exp(jnp.sum(lq2_ref[...] * lk2_ref[...], axis=-1, keepdims=True)) + lam_init)
    o = acc_ref[:, :hd] / acc_ref[:, hd:]
    o = o[:tq] - lam * o[tq:]
    o_ref[...] = (_rms(o, sg_ref[...]) * (1.0 - lam_init)).astype(BF16)


def _diff_attention(qkv, lq1, lk1, lq2, lk2, subln_g, batch, seq, lam_init, tq):
    t = qkv.shape[0]
    nq = seq // tq
    hd = V_HEAD_DIM
    small = pl.BlockSpec((1, HEAD_DIM), lambda b, h, i: (0, 0))
    return pl.pallas_call(
        functools.partial(_attn_kernel, tq=tq, lam_init=lam_init),
        out_shape=jax.ShapeDtypeStruct((t, N_HEADS * hd), BF16),
        grid=(batch, N_HEADS, nq),
        in_specs=[
            pl.BlockSpec((tq, hd), lambda b, h, i: (b * nq + i, h)),
            pl.BlockSpec((seq, hd), lambda b, h, i: (b, N_HEADS + h)),
            pl.BlockSpec((seq, hd), lambda b, h, i: (b, 2 * N_HEADS + h)),
            small, small, small, small,
            pl.BlockSpec((1, hd), lambda b, h, i: (0, 0)),
        ],
        out_specs=pl.BlockSpec((tq, hd), lambda b, h, i: (b * nq + i, h)),
        scratch_shapes=[pltpu.VMEM((nq, 2 * tq, tq), F32), pltpu.VMEM((2 * tq, hd), F32),
                        pltpu.VMEM((2 * tq, hd), F32), pltpu.VMEM((2 * tq, 2 * hd), F32),
                        pltpu.VMEM((seq, 2 * hd), BF16), pltpu.VMEM((2 * tq, tq), F32)],
        compiler_params=_params("arbitrary", "arbitrary", "arbitrary"),
        name="diff_attn",
    )(qkv, qkv, qkv, lq1, lk1, lq2, lk2, subln_g)


def _rglru_kernel(x_ref, g1_ref, wxy_ref, cw_ref, cb_ref, wa_ref, ba_ref, wx_ref, bx_ref, lam_ref, o_ref,
                  xbuf_ref, h_ref, *, ts, d_rnn):
    s_idx = pl.program_id(1)

    @pl.when(s_idx == 0)
    def _():
        xbuf_ref[0:SUBLANES, :] = jnp.zeros((SUBLANES, d_rnn), F32)
        h_ref[...] = jnp.zeros(h_ref.shape, F32)

    h_in = _rms(x_ref[...], g1_ref[...]).astype(BF16)
    xy = _dot(h_in, wxy_ref[...])
    xr = xy[:, :d_rnn]
    yr = xy[:, d_rnn:]

    xbuf_ref[SUBLANES:SUBLANES + ts, :] = xr
    conv = cb_ref[...] + jnp.zeros((ts, d_rnn), F32)
    for tap in range(CONV_WIDTH):
        off = SUBLANES - (CONV_WIDTH - 1) + tap
        conv = conv + xbuf_ref[off:off + ts, :] * cw_ref[tap:tap + 1, :]
    xbuf_ref[0:SUBLANES, :] = xbuf_ref[ts:ts + SUBLANES, :]

    cbf = conv.astype(BF16)
    ga, gi = [], []
    for g in range(d_rnn // MXU_TILE):
        sl = slice(g * MXU_TILE, (g + 1) * MXU_TILE)
        ga.append(_dot(cbf[:, sl], wa_ref[g]))
        gi.append(_dot(cbf[:, sl], wx_ref[g]))
    gate_r = jax.nn.sigmoid(jnp.concatenate(ga, axis=-1) + ba_ref[...])
    gate_i = jax.nn.sigmoid(jnp.concatenate(gi, axis=-1) + bx_ref[...])

    neg_lam = -lam_ref[...]
    softplus = jnp.maximum(neg_lam, 0.0) + jnp.log1p(jnp.exp(-jnp.abs(neg_lam)))
    log_a = -RG_C * gate_r * softplus
    a = jnp.exp(log_a)
    th = jnp.tanh(log_a)
    mult = jnp.sqrt(-2.0 * th / (1.0 - th))
    u = mult * (gate_i * conv)

    groups = ts // SUBLANES
    a3 = a.reshape(groups, SUBLANES, d_rnn)
    u3 = u.reshape(groups, SUBLANES, d_rnn)
    sub = lax.broadcasted_iota(jnp.int32, a3.shape, 1)
    d = 1
    while d < SUBLANES:
        keep = sub >= d
        a_sh = jnp.where(keep, pltpu.roll(a3, d, 1), 1.0)
        u_sh = jnp.where(keep, pltpu.roll(u3, d, 1), 0.0)
        u3 = a3 * u_sh + u3
        a3 = a3 * a_sh
        d *= 2
    carry = h_ref[...]
    hs = []
    for g in range(groups):
        hg = u3[g] + a3[g] * carry
        hs.append(hg)
        carry = hg[SUBLANES - 1:SUBLANES, :]
    h_ref[...] = carry
    o_ref[...] = (jnp.concatenate(hs, axis=0) * jax.nn.gelu(yr)).astype(BF16)


def _rglru(x2d, g1, wxy, conv_w, conv_b, wa_bd, b_a, wx_bd, b_x, rg_lambda, batch, seq, ts):
    t, d = x2d.shape
    d_rnn = conv_w.shape[1]
    ns = seq // ts
    ng = d_rnn // MXU_TILE
    const2 = lambda b, s: (0, 0)
    const3 = lambda b, s: (0, 0, 0)
    return pl.pallas_call(
        functools.partial(_rglru_kernel, ts=ts, d_rnn=d_rnn),
        out_shape=jax.ShapeDtypeStruct((t, d_rnn), BF16),
        grid=(batch, ns),
        in_specs=[
            pl.BlockSpec((ts, d), lambda b, s: (b * ns + s, 0)),
            pl.BlockSpec((1, d), const2),
            pl.BlockSpec((d, 2 * d_rnn), const2),
            pl.BlockSpec((CONV_WIDTH, d_rnn), const2),
            pl.BlockSpec((1, d_rnn), const2),
            pl.BlockSpec((ng, MXU_TILE, MXU_TILE), const3),
            pl.BlockSpec((1, d_rnn), const2),
            pl.BlockSpec((ng, MXU_TILE, MXU_TILE), const3),
            pl.BlockSpec((1, d_rnn), const2),
            pl.BlockSpec((1, d_rnn), const2),
        ],
        out_specs=pl.BlockSpec((ts, d_rnn), lambda b, s: (b * ns + s, 0)),
        scratch_shapes=[pltpu.VMEM((ts + SUBLANES, d_rnn), F32), pltpu.VMEM((1, d_rnn), F32)],
        compiler_params=_params("parallel", "arbitrary"),
        name="rglru",
    )(x2d, g1, wxy, conv_w, conv_b, wa_bd, b_a, wx_bd, b_x, rg_lambda)


def _merge_kernel(x_ref, attn_ref, rnn_ref, g1_ref, wg_ref, bg_ref, pa_ref, pr_ref, wo_ref, g2_ref,
                  x2_ref, xn_ref, *, d):
    x = x_ref[...]
    h = _rms(x, g1_ref[...]).astype(BF16)
    gates = jax.nn.sigmoid(_dot(h, wg_ref[...]) + bg_ref[...])
    merged = gates[:, :d] * _dot(attn_ref[...], pa_ref[...]) + gates[:, d:] * _dot(rnn_ref[...], pr_ref[...])
    x2 = x + _dot(merged.astype(BF16), wo_ref[...])
    x2_ref[...] = x2
    xn_ref[...] = _rms(x2, g2_ref[...]).T.astype(BF16)


def _merge(x2d, attn, rnn, g1, wg, bg, pa, pr, wo, g2, tm):
    t, d = x2d.shape
    d_rnn = rnn.shape[1]
    const = lambda i: (0, 0)
    row = lambda i: (i, 0)
    return pl.pallas_call(
        functools.partial(_merge_kernel, d=d),
        out_shape=(jax.ShapeDtypeStruct((t, d), F32), jax.ShapeDtypeStruct((d, t), BF16)),
        grid=(t // tm,),
        in_specs=[
            pl.BlockSpec((tm, d), row), pl.BlockSpec((tm, d), row), pl.BlockSpec((tm, d_rnn), row),
            pl.BlockSpec((1, d), const), pl.BlockSpec((d, 2 * d), const), pl.BlockSpec((1, 2 * d), const),
            pl.BlockSpec((d, d), const), pl.BlockSpec((d_rnn, d), const), pl.BlockSpec((d, d), const),
            pl.BlockSpec((1, d), const),
        ],
        out_specs=(pl.BlockSpec((tm, d), row), pl.BlockSpec((d, tm), lambda i: (0, i))),
        compiler_params=_params("parallel"),
        name="merge",
    )(x2d, attn, rnn, g1, wg, bg, pa, pr, wo, g2)


def _top_values(s, k):
    vals = []
    cur = s
    for r in range(k):
        m = jnp.max(cur, axis=0, keepdims=True)
        vals.append(m)
        if r + 1 < k:
            cur = jnp.where(cur == m, -jnp.inf, cur)
    return vals


def _merge_exchange_network(n):
    pairs = []
    p = 1
    while p < n:
        k = p
        while k >= 1:
            for j in range(k % p, n - k, 2 * k):
                for i in range(min(k, n - j - k)):
                    if (i + j) // (2 * p) == (i + j + k) // (2 * p):
                        pairs.append((i + j, i + j + k))
            k //= 2
        p *= 2
    return pairs


def _top_values_of_keys(s, k):
    groups = s.shape[0] // SUBLANES
    assert groups == k
    stack = [s[g * SUBLANES:(g + 1) * SUBLANES, :] for g in range(groups)]
    for i, j in _merge_exchange_network(groups):
        stack[i], stack[j] = jnp.maximum(stack[i], stack[j]), jnp.minimum(stack[i], stack[j])
    vals = []
    for r in range(k):
        m = jnp.max(stack[0], axis=0, keepdims=True)
        vals.append(m)
        if r + 1 < k:
            top = stack[0] == m
            for level in range(k - 1 - r):
                stack[level] = jnp.where(top, stack[level + 1], stack[level])
    return vals


def _route_kernel(xn_ref, wqt_ref, keys_ref, n_ref, e1_ref, rank2_ref, e2_ref, *, tm):
    qt = _dot(wqt_ref[...], xn_ref[...]).astype(BF16)
    nk = PEER_N_KEYS
    for h in range(PEER_HEADS):
        s1 = _dot(keys_ref[2 * h], qt[(2 * h) * nk:(2 * h + 1) * nk, :])
        s2 = _dot(keys_ref[2 * h + 1], qt[(2 * h + 1) * nk:(2 * h + 2) * nk, :])
        v1 = _top_values_of_keys(s1, PEER_TOPK)
        v2 = _top_values_of_keys(s2, PEER_TOPK)
        rank2 = jnp.full(s2.shape, float(PEER_TOPK), F32)
        for j in reversed(range(PEER_TOPK)):
            rank2 = jnp.where(s2 >= v2[j], float(j), rank2)
        cands = [v1[i] + v2[j] for i in range(PEER_TOPK) for j in range(PEER_TOPK)
                 if (i + 1) * (j + 1) <= PEER_TOPK]
        pad = (-len(cands)) % SUBLANES
        cands += [jnp.full((1, tm), -jnp.inf, F32)] * pad
        best = _top_values(jnp.concatenate(cands, axis=0), PEER_TOPK)
        tau = best[PEER_TOPK - 1]
        z = sum(jnp.exp(b - best[0]) for b in best)
        v1_rows = jnp.concatenate(v1, axis=0)
        m = sum(jnp.where(v1_rows + v2[j] >= tau, 1.0, 0.0) for j in range(PEER_TOPK))
        n = jnp.zeros(s1.shape, F32)
        for i in range(PEER_TOPK):
            n = jnp.where(s1 == v1[i], m[i:i + 1, :], n)
        n_ref[h] = n
        e1_ref[h] = 0.5 * jnp.exp(s1 - v1[0]) / z
        rank2_ref[h] = rank2.astype(BF16)
        e2_ref[h] = jnp.exp(s2 - v2[0]).astype(BF16)


def _peer_route(xn, wqt, keys, tm):
    d, t = xn.shape
    nk = PEER_N_KEYS
    words = jax.ShapeDtypeStruct((PEER_HEADS, nk, t), F32)
    halves = jax.ShapeDtypeStruct((PEER_HEADS, nk, t), BF16)
    ospec = pl.BlockSpec((PEER_HEADS, nk, tm), lambda i: (0, 0, i))
    return pl.pallas_call(
        functools.partial(_route_kernel, tm=tm),
        out_shape=(words, words, halves, halves),
        grid=(t // tm,),
        in_specs=[
            pl.BlockSpec((d, tm), lambda i: (0, i)),
            pl.BlockSpec(wqt.shape, lambda i: (0, 0)),
            pl.BlockSpec(keys.shape, lambda i: (0, 0, 0)),
        ],
        out_specs=(ospec, ospec, ospec, ospec),
        compiler_params=_params("parallel"),
        name="peer_route",
    )(xn, wqt, keys)


GELU_C0 = math.sqrt(2.0 / math.pi)
GELU_C1 = GELU_C0 * 0.044715


def _peer_dense_kernel(xn_ref, x2_ref, u_ref, vt_ref, n_ref, e1_ref, rank2_ref, e2_ref, o_ref,
                       acc_ref, s_ref, w_ref, *, na):
    j = pl.program_id(1)
    nk = PEER_N_KEYS
    tm = s_ref.shape[1]

    @pl.when(j == 0)
    def _():
        acc_ref[...] = jnp.zeros(acc_ref.shape, F32)

    s_ref[...] = _dot(u_ref[...], xn_ref[...])

    for a in range(na):
        rows = slice(a * nk, (a + 1) * nk)
        coef = None
        for h in range(PEER_HEADS):
            n_b = jnp.broadcast_to(n_ref[h, a:a + 1, :].astype(BF16), (nk, tm))
            e_b = jnp.broadcast_to(e1_ref[h, a:a + 1, :].astype(BF16), (nk, tm))
            term = e_b * jnp.where(rank2_ref[h] < n_b, e2_ref[h], jnp.zeros((), BF16))
            coef = term if coef is None else coef + term
        x = s_ref[rows, :]
        th = jnp.tanh(x * (GELU_C0 + GELU_C1 * (x * x)))
        w_ref[rows, :] = coef * (x + x * th).astype(BF16)
    acc_ref[...] += _dot(vt_ref[...], w_ref[...])

    @pl.when(j == pl.num_programs(1) - 1)
    def _():
        o_ref[...] = x2_ref[...] + acc_ref[...].T


def _peer_dense(xn, x2, u_bf, vt_bf, n_w, e1_w, rank2, e2, tm, na):
    d, t = xn.shape
    n_exp = u_bf.shape[0]
    nk = PEER_N_KEYS
    te = na * nk
    return pl.pallas_call(
        functools.partial(_peer_dense_kernel, na=na),
        out_shape=jax.ShapeDtypeStruct((t, d), F32),
        grid=(t // tm, n_exp // te),
        in_specs=[
            pl.BlockSpec((d, tm), lambda i, j: (0, i)),
            pl.BlockSpec((tm, d), lambda i, j: (i, 0)),
            pl.BlockSpec((te, d), lambda i, j: (j, 0)),
            pl.BlockSpec((d, te), lambda i, j: (0, j)),
            pl.BlockSpec((PEER_HEADS, na, tm), lambda i, j: (0, j, i)),
            pl.BlockSpec((PEER_HEADS, na, tm), lambda i, j: (0, j, i)),
            pl.BlockSpec((PEER_HEADS, nk, tm), lambda i, j: (0, 0, i)),
            pl.BlockSpec((PEER_HEADS, nk, tm), lambda i, j: (0, 0, i)),
        ],
        out_specs=pl.BlockSpec((tm, d), lambda i, j: (i, 0)),
        scratch_shapes=[pltpu.VMEM((d, tm), F32), pltpu.VMEM((te, tm), F32), pltpu.VMEM((te, tm), BF16)],
        compiler_params=_params("parallel", "arbitrary"),
        name="peer_dense",
    )(xn, x2, u_bf, vt_bf, n_w, e1_w, rank2, e2)


def _block_diag_tiles(w):
    nb, r, _ = w.shape
    per = MXU_TILE // r
    w = w.reshape(nb // per, per, r, r)
    eye = jnp.eye(per, dtype=w.dtype)
    return jnp.einsum("gpij,pq->gpiqj", w, eye).reshape(nb // per, MXU_TILE, MXU_TILE)


def _layer(x, lam_init, norm1_g, w_in, b_gate, q_norm_g, k_norm_g, lambda_q1, lambda_k1, lambda_q2,
           lambda_k2, subln_g, conv_w, conv_b, w_rg_a, b_rg_a, w_rg_x, b_rg_x, rg_lambda, w_br_attn,
           w_br_rnn, w_out, norm2_g, w_peer_q, peer_sub_keys, peer_u, peer_v):
    batch, seq, d = x.shape
    t = batch * seq
    d_rnn = conv_w.shape[1]
    x2d = x.reshape(t, d)
    row = lambda v: v.reshape(1, -1)

    w_in_bf = w_in.astype(BF16)
    n_grp = d // HEAD_DIM
    scale = HEAD_DIM ** -0.5
    gn = jnp.stack([jnp.tile(q_norm_g, n_grp) * (scale * LOG2_E), jnp.tile(k_norm_g, n_grp),
                    jnp.ones((d,), F32)]).reshape(3, 1, d)
    gsum = jnp.kron(jnp.eye(MXU_TILE // HEAD_DIM, dtype=F32),
                    jnp.full((HEAD_DIM, HEAD_DIM), 1.0 / HEAD_DIM, F32)).astype(BF16)
    wxy = w_in_bf[:, 3 * d:3 * d + 2 * d_rnn]
    wg = w_in_bf[:, 3 * d + 2 * d_rnn:]
    wa_bd = _block_diag_tiles(w_rg_a).astype(BF16)
    wx_bd = _block_diag_tiles(w_rg_x).astype(BF16)
    wqt = w_peer_q.T.astype(BF16)
    keys = peer_sub_keys.reshape(PEER_HEADS * 2, PEER_N_KEYS, -1).astype(BF16)
    u_bf = peer_u.astype(BF16)
    vt_bf = peer_v.T.astype(BF16)

    g1 = row(norm1_g)
    qkv = _qkv_proj(x2d, g1, w_in_bf, gn, gsum, tm=512)
    attn = _diff_attention(qkv, row(lambda_q1), row(lambda_k1), row(lambda_q2), row(lambda_k2),
                           row(subln_g), batch, seq, lam_init, tq=512)
    rnn = _rglru(x2d, g1, wxy, conv_w, row(conv_b), wa_bd, row(b_rg_a), wx_bd, row(b_rg_x),
                 row(rg_lambda), batch, seq, ts=256)
    x2, xn = _merge(x2d, attn, rnn, g1, wg, row(b_gate), w_br_attn.astype(BF16), w_br_rnn.astype(BF16),
                    w_out.astype(BF16), row(norm2_g), tm=256)
    n_w, e1_w, rank2, e2 = _peer_route(xn, wqt, keys, tm=256)
    out = _peer_dense(xn, x2, u_bf, vt_bf, n_w, e1_w, rank2, e2, tm=512, na=16)
    return out.reshape(batch, seq, d)


def kernel(x, norm1_g, w_in, b_gate, q_norm_g, k_norm_g, lambda_q1, lambda_k1, lambda_q2, lambda_k2,
           subln_g, conv_w, conv_b, w_rg_a, b_rg_a, w_rg_x, b_rg_x, rg_lambda, w_br_attn, w_br_rnn,
           w_out, norm2_g, w_peer_q, peer_sub_keys, peer_u, peer_v):
    depth = norm1_g.shape[0]
    for layer in range(depth):
        lam_init = 0.8 - 0.6 * math.exp(-0.3 * layer)
        x = _layer(x, lam_init, norm1_g[layer], w_in[layer], b_gate[layer], q_norm_g[layer],
                   k_norm_g[layer], lambda_q1[layer], lambda_k1[layer], lambda_q2[layer],
                   lambda_k2[layer], subln_g[layer], conv_w[layer], conv_b[layer], w_rg_a[layer],
                   b_rg_a[layer], w_rg_x[layer], b_rg_x[layer], rg_lambda[layer], w_br_attn[layer],
                   w_br_rnn[layer], w_out[layer], norm2_g[layer], w_peer_q[layer], peer_sub_keys[layer],
                   peer_u[layer], peer_v[layer])
    return x
```

```python
import functools
import math
from typing import NamedTuple

import jax
import jax.numpy as jnp
from jax import lax
from jax.experimental import pallas as pl
from jax.experimental.pallas import tpu as pltpu

F32 = jnp.float32
BF16 = jnp.bfloat16

EPS = 1e-6
NEG_INF = -1e30
LOG2_E = math.log2(math.e)
CHUNK = 64
N_HEADS = 8
HEAD_DIM = 64
V_HEAD_DIM = 128
CONV_WIDTH = 4
RG_C = 8.0
PEER_HEADS = 8
PEER_N_KEYS = 128
PEER_TOPK = 16

MXU_TILE = 256
SUBLANES = 8
VMEM_V7X = 64 * 1024 * 1024
VMEM_LIMIT = VMEM_V7X * 7 // 8


class Tiles(NamedTuple):
    qkv: int = 512
    attn_q: int = 512
    rnn: int = 256
    merge: int = 256
    route: int = 256
    dense: int = 512
    dense_keys: int = 16


TILES = Tiles()


def _params(*sem):
    return pltpu.CompilerParams(dimension_semantics=sem, vmem_limit_bytes=VMEM_LIMIT)


def _rms(x, g):
    ms = jnp.mean(x * x, axis=-1, keepdims=True)
    return x * lax.rsqrt(ms + EPS) * g


def _dot(a, b):
    return jnp.dot(a, b, preferred_element_type=F32)


def _dot_nt(a, b):
    return lax.dot_general(a, b, (((1,), (1,)), ((), ())), preferred_element_type=F32)


def _qkv_kernel(x_ref, g1_ref, w_ref, gn_ref, gsum_ref, o_ref, h_ref):
    j = pl.program_id(1)

    @pl.when(j == 0)
    def _():
        h_ref[...] = _rms(x_ref[...], g1_ref[...]).astype(BF16)

    y = _dot(h_ref[...], w_ref[...])

    @pl.when(j < 2)
    def _():
        y2 = y * y
        hi = y2.astype(BF16)
        lo = (y2 - hi.astype(F32)).astype(BF16)
        g = gsum_ref[...]
        ms = jnp.concatenate(
            [_dot(hi[:, c:c + MXU_TILE], g) + _dot(lo[:, c:c + MXU_TILE], g)
             for c in range(0, y.shape[1], MXU_TILE)], axis=-1)
        o_ref[...] = (y * lax.rsqrt(ms + EPS) * gn_ref[0]).astype(BF16)

    @pl.when(j == 2)
    def _():
        o_ref[...] = y.astype(BF16)


def _qkv_proj(x2d, g1, w_in_bf, gn, gsum, tm):
    t, d = x2d.shape
    return pl.pallas_call(
        _qkv_kernel,
        out_shape=jax.ShapeDtypeStruct((t, 3 * d), BF16),
        grid=(t // tm, 3),
        in_specs=[
            pl.BlockSpec((tm, d), lambda i, j: (i, 0)),
            pl.BlockSpec((1, d), lambda i, j: (0, 0)),
            pl.BlockSpec((d, d), lambda i, j: (0, j)),
            pl.BlockSpec((1, 1, d), lambda i, j: (j, 0, 0)),
            pl.BlockSpec((MXU_TILE, MXU_TILE), lambda i, j: (0, 0)),
        ],
        out_specs=pl.BlockSpec((tm, d), lambda i, j: (i, j)),
        scratch_shapes=[pltpu.VMEM((tm, d), BF16)],
        compiler_params=_params("parallel", "arbitrary"),
        name="qkv_proj",
    )(x2d, g1, w_in_bf, gn, gsum)


def _attn_kernel(q_ref, k_ref, v_ref, lq1_ref, lk1_ref, lq2_ref, lk2_ref, sg_ref, o_ref,
                 s_buf, mrun_ref, m_ref, acc_ref, vx_ref, bias_ref, *, tq, lam_init):
    i = pl.program_id(2)
    hd = V_HEAD_DIM

    @pl.when((pl.program_id(0) == 0) & (pl.program_id(1) == 0) & (i == 0))
    def _():
        r = lax.broadcasted_iota(jnp.int32, bias_ref.shape, 0)
        c = lax.broadcasted_iota(jnp.int32, bias_ref.shape, 1)
        qpos = jnp.where(r >= tq, r - tq, r)
        bias_ref[...] = jnp.where((c // CHUNK) <= (qpos // CHUNK), 0.0, NEG_INF)

    @pl.when(i == 0)
    def _():
        vx_ref[:, :hd] = v_ref[...]
        vx_ref[:, hd:] = jnp.ones((vx_ref.shape[0], hd), BF16)

    q = q_ref[...]
    lane = lax.broadcasted_iota(jnp.int32, q.shape, 1)
    zero = jnp.zeros_like(q)
    qs = jnp.concatenate([jnp.where(lane < HEAD_DIM, q, zero), jnp.where(lane >= HEAD_DIM, q, zero)], axis=0)

    mrun_ref[...] = jnp.full(mrun_ref.shape, NEG_INF, F32)

    def scores(j, masked):
        start = pl.multiple_of(j * tq, tq)
        s = _dot_nt(qs, k_ref[pl.ds(start, tq), :])
        if masked:
            s = s + bias_ref[...]
        s_buf[j] = s
        m = mrun_ref[...]
        for c0 in range(0, tq, hd):
            m = jnp.maximum(m, s[:, c0:c0 + hd])
        mrun_ref[...] = m

    def full_scores(p, carry):
        scores(2 * p, False)
        scores(2 * p + 1, False)
        return carry

    lax.fori_loop(0, i // 2, full_scores, 0)

    @pl.when(i % 2 == 1)
    def _():
        scores(i - 1, False)

    scores(i, True)
    m_ref[...] = jnp.broadcast_to(jnp.max(mrun_ref[...], axis=-1, keepdims=True), m_ref.shape)

    acc_ref[...] = jnp.zeros(acc_ref.shape, F32)

    def weighted(j):
        start = pl.multiple_of(j * tq, tq)
        s = s_buf[j]
        m = m_ref[...]
        p = jnp.concatenate([jnp.exp2(s[:, c0:c0 + hd] - m) for c0 in range(0, tq, hd)], axis=-1)
        return _dot(p.astype(BF16), vx_ref[pl.ds(start, tq), :])

    def pair_weighted(p, carry):
        acc_ref[...] += weighted(2 * p) + weighted(2 * p + 1)
        return carry

    lax.fori_loop(0, (i + 1) // 2, pair_weighted, 0)

    @pl.when(i % 2 == 0)
    def _():
        acc_ref[...] += weighted(i)

    lam = (jnp.exp(jnp.sum(lq1_ref[...] * lk1_ref[...], axis=-1, keepdims=True))
           - jnp.exp(jnp.sum(lq2_ref[...] * lk2_ref[...], axis=-1, keepdims=True)) + lam_init)
    o = acc_ref[:, :hd] / acc_ref[:, hd:]
    o = o[:tq] - lam * o[tq:]
    o_ref[...] = (_rms(o, sg_ref[...]) * (1.0 - lam_init)).astype(BF16)


def _diff_attention(qkv, lq1, lk1, lq2, lk2, subln_g, batch, seq, lam_init, tq):
    t = qkv.shape[0]
    nq = seq // tq
    hd = V_HEAD_DIM
    small = pl.BlockSpec((1, HEAD_DIM), lambda b, h, i: (0, 0))
    return pl.pallas_call(
        functools.partial(_attn_kernel, tq=tq, lam_init=lam_init),
        out_shape=jax.ShapeDtypeStruct((t, N_HEADS * hd), BF16),
        grid=(batch, N_HEADS, nq),
        in_specs=[
            pl.BlockSpec((tq, hd), lambda b, h, i: (b * nq + i, h)),
            pl.BlockSpec((seq, hd), lambda b, h, i: (b, N_HEADS + h)),
            pl.BlockSpec((seq, hd), lambda b, h, i: (b, 2 * N_HEADS + h)),
            small, small, small, small,
            pl.BlockSpec((1, hd), lambda b, h, i: (0, 0)),
        ],
        out_specs=pl.BlockSpec((tq, hd), lambda b, h, i: (b * nq + i, h)),
        scratch_shapes=[pltpu.VMEM((nq, 2 * tq, tq), F32), pltpu.VMEM((2 * tq, hd), F32),
                        pltpu.VMEM((2 * tq, hd), F32), pltpu.VMEM((2 * tq, 2 * hd), F32),
                        pltpu.VMEM((seq, 2 * hd), BF16), pltpu.VMEM((2 * tq, tq), F32)],
        compiler_params=_params("arbitrary", "arbitrary", "arbitrary"),
        name="diff_attn",
    )(qkv, qkv, qkv, lq1, lk1, lq2, lk2, subln_g)


def _rglru_kernel(x_ref, g1_ref, wxy_ref, cw_ref, cb_ref, wa_ref, ba_ref, wx_ref, bx_ref, lam_ref, o_ref,
                  xbuf_ref, h_ref, *, ts, d_rnn):
    s_idx = pl.program_id(1)

    @pl.when(s_idx == 0)
    def _():
        xbuf_ref[0:SUBLANES, :] = jnp.zeros((SUBLANES, d_rnn), F32)
        h_ref[...] = jnp.zeros(h_ref.shape, F32)

    h_in = _rms(x_ref[...], g1_ref[...]).astype(BF16)
    xy = _dot(h_in, wxy_ref[...])
    xr = xy[:, :d_rnn]
    yr = xy[:, d_rnn:]

    xbuf_ref[SUBLANES:SUBLANES + ts, :] = xr
    conv = cb_ref[...] + jnp.zeros((ts, d_rnn), F32)
    for tap in range(CONV_WIDTH):
        off = SUBLANES - (CONV_WIDTH - 1) + tap
        conv = conv + xbuf_ref[off:off + ts, :] * cw_ref[tap:tap + 1, :]
    xbuf_ref[0:SUBLANES, :] = xbuf_ref[ts:ts + SUBLANES, :]

    cbf = conv.astype(BF16)
    ga, gi = [], []
    for g in range(d_rnn // MXU_TILE):
        sl = slice(g * MXU_TILE, (g + 1) * MXU_TILE)
        ga.append(_dot(cbf[:, sl], wa_ref[g]))
        gi.append(_dot(cbf[:, sl], wx_ref[g]))
    gate_r = jax.nn.sigmoid(jnp.concatenate(ga, axis=-1) + ba_ref[...])
    gate_i = jax.nn.sigmoid(jnp.concatenate(gi, axis=-1) + bx_ref[...])

    neg_lam = -lam_ref[...]
    softplus = jnp.maximum(neg_lam, 0.0) + jnp.log1p(jnp.exp(-jnp.abs(neg_lam)))
    log_a = -RG_C * gate_r * softplus
    a = jnp.exp(log_a)
    th = jnp.tanh(log_a)
    mult = jnp.sqrt(-2.0 * th / (1.0 - th))
    u = mult * (gate_i * conv)

    groups = ts // SUBLANES
    a3 = a.reshape(groups, SUBLANES, d_rnn)
    u3 = u.reshape(groups, SUBLANES, d_rnn)
    sub = lax.broadcasted_iota(jnp.int32, a3.shape, 1)
    d = 1
    while d < SUBLANES:
        keep = sub >= d
        a_sh = jnp.where(keep, pltpu.roll(a3, d, 1), 1.0)
        u_sh = jnp.where(keep, pltpu.roll(u3, d, 1), 0.0)
        u3 = a3 * u_sh + u3
        a3 = a3 * a_sh
        d *= 2
    carry = h_ref[...]
    hs = []
    for g in range(groups):
        hg = u3[g] + a3[g] * carry
        hs.append(hg)
        carry = hg[SUBLANES - 1:SUBLANES, :]
    h_ref[...] = carry
    o_ref[...] = (jnp.concatenate(hs, axis=0) * jax.nn.gelu(yr)).astype(BF16)


def _rglru(x2d, g1, wxy, conv_w, conv_b, wa_bd, b_a, wx_bd, b_x, rg_lambda, batch, seq, ts):
    t, d = x2d.shape
    d_rnn = conv_w.shape[1]
    ns = seq // ts
    ng = d_rnn // MXU_TILE
    const2 = lambda b, s: (0, 0)
    const3 = lambda b, s: (0, 0, 0)
    return pl.pallas_call(
        functools.partial(_rglru_kernel, ts=ts, d_rnn=d_rnn),
        out_shape=jax.ShapeDtypeStruct((t, d_rnn), BF16),
        grid=(batch, ns),
        in_specs=[
            pl.BlockSpec((ts, d), lambda b, s: (b * ns + s, 0)),
            pl.BlockSpec((1, d), const2),
            pl.BlockSpec((d, 2 * d_rnn), const2),
            pl.BlockSpec((CONV_WIDTH, d_rnn), const2),
            pl.BlockSpec((1, d_rnn), const2),
            pl.BlockSpec((ng, MXU_TILE, MXU_TILE), const3),
            pl.BlockSpec((1, d_rnn), const2),
            pl.BlockSpec((ng, MXU_TILE, MXU_TILE), const3),
            pl.BlockSpec((1, d_rnn), const2),
            pl.BlockSpec((1, d_rnn), const2),
        ],
        out_specs=pl.BlockSpec((ts, d_rnn), lambda b, s: (b * ns + s, 0)),
        scratch_shapes=[pltpu.VMEM((ts + SUBLANES, d_rnn), F32), pltpu.VMEM((1, d_rnn), F32)],
        compiler_params=_params("parallel", "arbitrary"),
        name="rglru",
    )(x2d, g1, wxy, conv_w, conv_b, wa_bd, b_a, wx_bd, b_x, rg_lambda)


def _merge_kernel(x_ref, attn_ref, rnn_ref, g1_ref, wg_ref, bg_ref, pa_ref, pr_ref, wo_ref, g2_ref,
                  x2_ref, xn_ref, *, d):
    x = x_ref[...]
    h = _rms(x, g1_ref[...]).astype(BF16)
    gates = jax.nn.sigmoid(_dot(h, wg_ref[...]) + bg_ref[...])
    merged = gates[:, :d] * _dot(attn_ref[...], pa_ref[...]) + gates[:, d:] * _dot(rnn_ref[...], pr_ref[...])
    x2 = x + _dot(merged.astype(BF16), wo_ref[...])
    x2_ref[...] = x2
    xn_ref[...] = _rms(x2, g2_ref[...]).T.astype(BF16)


def _merge(x2d, attn, rnn, g1, wg, bg, pa, pr, wo, g2, tm):
    t, d = x2d.shape
    d_rnn = rnn.shape[1]
    const = lambda i: (0, 0)
    row = lambda i: (i, 0)
    return pl.pallas_call(
        functools.partial(_merge_kernel, d=d),
        out_shape=(jax.ShapeDtypeStruct((t, d), F32), jax.ShapeDtypeStruct((d, t), BF16)),
        grid=(t // tm,),
        in_specs=[
            pl.BlockSpec((tm, d), row), pl.BlockSpec((tm, d), row), pl.BlockSpec((tm, d_rnn), row),
            pl.BlockSpec((1, d), const), pl.BlockSpec((d, 2 * d), const), pl.BlockSpec((1, 2 * d), const),
            pl.BlockSpec((d, d), const), pl.BlockSpec((d_rnn, d), const), pl.BlockSpec((d, d), const),
            pl.BlockSpec((1, d), const),
        ],
        out_specs=(pl.BlockSpec((tm, d), row), pl.BlockSpec((d, tm), lambda i: (0, i))),
        compiler_params=_params("parallel"),
        name="merge",
    )(x2d, attn, rnn, g1, wg, bg, pa, pr, wo, g2)


def _top_values(s, k):
    vals = []
    cur = s
    for r in range(k):
        m = jnp.max(cur, axis=0, keepdims=True)
        vals.append(m)
        if r + 1 < k:
            cur = jnp.where(cur == m, -jnp.inf, cur)
    return vals


def _merge_exchange_network(n):
    pairs = []
    p = 1
    while p < n:
        k = p
        while k >= 1:
            for j in range(k % p, n - k, 2 * k):
                for i in range(min(k, n - j - k)):
                    if (i + j) // (2 * p) == (i + j + k) // (2 * p):
                        pairs.append((i + j, i + j + k))
            k //= 2
        p *= 2
    return pairs


def _top_values_of_keys(s, k):
    groups = s.shape[0] // SUBLANES
    assert groups == k
    stack = [s[g * SUBLANES:(g + 1) * SUBLANES, :] for g in range(groups)]
    for i, j in _merge_exchange_network(groups):
        stack[i], stack[j] = jnp.maximum(stack[i], stack[j]), jnp.minimum(stack[i], stack[j])
    vals = []
    for r in range(k):
        m = jnp.max(stack[0], axis=0, keepdims=True)
        vals.append(m)
        if r + 1 < k:
            top = stack[0] == m
            for level in range(k - 1 - r):
                stack[level] = jnp.where(top, stack[level + 1], stack[level])
    return vals


def _route_kernel(xn_ref, wqt_ref, keys_ref, n_ref, e1_ref, rank2_ref, e2_ref, *, tm):
    qt = _dot(wqt_ref[...], xn_ref[...]).astype(BF16)
    nk = PEER_N_KEYS
    for h in range(PEER_HEADS):
        s1 = _dot(keys_ref[2 * h], qt[(2 * h) * nk:(2 * h + 1) * nk, :])
        s2 = _dot(keys_ref[2 * h + 1], qt[(2 * h + 1) * nk:(2 * h + 2) * nk, :])
        v1 = _top_values_of_keys(s1, PEER_TOPK)
        v2 = _top_values_of_keys(s2, PEER_TOPK)
        rank2 = jnp.full(s2.shape, float(PEER_TOPK), F32)
        for j in reversed(range(PEER_TOPK)):
            rank2 = jnp.where(s2 >= v2[j], float(j), rank2)
        cands = [v1[i] + v2[j] for i in range(PEER_TOPK) for j in range(PEER_TOPK)
                 if (i + 1) * (j + 1) <= PEER_TOPK]
        pad = (-len(cands)) % SUBLANES
        cands += [jnp.full((1, tm), -jnp.inf, F32)] * pad
        best = _top_values(jnp.concatenate(cands, axis=0), PEER_TOPK)
        tau = best[PEER_TOPK - 1]
        z = sum(jnp.exp(b - best[0]) for b in best)
        v1_rows = jnp.concatenate(v1, axis=0)
        m = sum(jnp.where(v1_rows + v2[j] >= tau, 1.0, 0.0) for j in range(PEER_TOPK))
        n = jnp.zeros(s1.shape, F32)
        for i in range(PEER_TOPK):
            n = jnp.where(s1 == v1[i], m[i:i + 1, :], n)
        n_ref[h] = n
        e1_ref[h] = 0.5 * jnp.exp(s1 - v1[0]) / z
        rank2_ref[h] = rank2.astype(BF16)
        e2_ref[h] = jnp.exp(s2 - v2[0]).astype(BF16)


def _peer_route(xn, wqt, keys, tm):
    d, t = xn.shape
    nk = PEER_N_KEYS
    words = jax.ShapeDtypeStruct((PEER_HEADS, nk, t), F32)
    halves = jax.ShapeDtypeStruct((PEER_HEADS, nk, t), BF16)
    ospec = pl.BlockSpec((PEER_HEADS, nk, tm), lambda i: (0, 0, i))
    return pl.pallas_call(
        functools.partial(_route_kernel, tm=tm),
        out_shape=(words, words, halves, halves),
        grid=(t // tm,),
        in_specs=[
            pl.BlockSpec((d, tm), lambda i: (0, i)),
            pl.BlockSpec(wqt.shape, lambda i: (0, 0)),
            pl.BlockSpec(keys.shape, lambda i: (0, 0, 0)),
        ],
        out_specs=(ospec, ospec, ospec, ospec),
        compiler_params=_params("parallel"),
        name="peer_route",
    )(xn, wqt, keys)


GELU_C0 = math.sqrt(2.0 / math.pi)
GELU_C1 = GELU_C0 * 0.044715


def _peer_dense_kernel(xn_ref, x2_ref, u_ref, vt_ref, n_ref, e1_ref, rank2_ref, e2_ref, o_ref,
                       acc_ref, s_ref, w_ref, *, na):
    j = pl.program_id(1)
    nk = PEER_N_KEYS
    tm = s_ref.shape[1]

    @pl.when(j == 0)
    def _():
        acc_ref[...] = jnp.zeros(acc_ref.shape, F32)

    s_ref[...] = _dot(u_ref[...], xn_ref[...])

    for a in range(na):
        rows = slice(a * nk, (a + 1) * nk)
        coef = None
        for h in range(PEER_HEADS):
            n_b = jnp.broadcast_to(n_ref[h, a:a + 1, :].astype(BF16), (nk, tm))
            e_b = jnp.broadcast_to(e1_ref[h, a:a + 1, :].astype(BF16), (nk, tm))
            term = e_b * jnp.where(rank2_ref[h] < n_b, e2_ref[h], jnp.zeros((), BF16))
            coef = term if coef is None else coef + term
        x = s_ref[rows, :]
        th = jnp.tanh(x * (GELU_C0 + GELU_C1 * (x * x)))
        w_ref[rows, :] = coef * (x + x * th).astype(BF16)
    acc_ref[...] += _dot(vt_ref[...], w_ref[...])

    @pl.when(j == pl.num_programs(1) - 1)
    def _():
        o_ref[...] = x2_ref[...] + acc_ref[...].T


def _peer_dense(xn, x2, u_bf, vt_bf, n_w, e1_w, rank2, e2, tm, na):
    d, t = xn.shape
    n_exp = u_bf.shape[0]
    nk = PEER_N_KEYS
    te = na * nk
    return pl.pallas_call(
        functools.partial(_peer_dense_kernel, na=na),
        out_shape=jax.ShapeDtypeStruct((t, d), F32),
        grid=(t // tm, n_exp // te),
        in_specs=[
            pl.BlockSpec((d, tm), lambda i, j: (0, i)),
            pl.BlockSpec((tm, d), lambda i, j: (i, 0)),
            pl.BlockSpec((te, d), lambda i, j: (j, 0)),
            pl.BlockSpec((d, te), lambda i, j: (0, j)),
            pl.BlockSpec((PEER_HEADS, na, tm), lambda i, j: (0, j, i)),
            pl.BlockSpec((PEER_HEADS, na, tm), lambda i, j: (0, j, i)),
            pl.BlockSpec((PEER_HEADS, nk, tm), lambda i, j: (0, 0, i)),
            pl.BlockSpec((PEER_HEADS, nk, tm), lambda i, j: (0, 0, i)),
        ],
        out_specs=pl.BlockSpec((tm, d), lambda i, j: (i, 0)),
        scratch_shapes=[pltpu.VMEM((d, tm), F32), pltpu.VMEM((te, tm), F32), pltpu.VMEM((te, tm), BF16)],
        compiler_params=_params("parallel", "arbitrary"),
        name="peer_dense",
    )(xn, x2, u_bf, vt_bf, n_w, e1_w, rank2, e2)


def _block_diag_tiles(w):
    nb, r, _ = w.shape
    per = MXU_TILE // r
    w = w.reshape(nb // per, per, r, r)
    eye = jnp.eye(per, dtype=w.dtype)
    return jnp.einsum("gpij,pq->gpiqj", w, eye).reshape(nb // per, MXU_TILE, MXU_TILE)


def _layer(x, lam_init, norm1_g, w_in, b_gate, q_norm_g, k_norm_g, lambda_q1, lambda_k1, lambda_q2,
           lambda_k2, subln_g, conv_w, conv_b, w_rg_a, b_rg_a, w_rg_x, b_rg_x, rg_lambda, w_br_attn,
           w_br_rnn, w_out, norm2_g, w_peer_q, peer_sub_keys, peer_u, peer_v):
    batch, seq, d = x.shape
    t = batch * seq
    d_rnn = conv_w.shape[1]
    x2d = x.reshape(t, d)
    row = lambda v: v.reshape(1, -1)
    assert d == N_HEADS * V_HEAD_DIM and d_rnn % MXU_TILE == 0 and MXU_TILE % w_rg_a.shape[1] == 0
    assert seq % TILES.attn_q == 0 and TILES.attn_q % CHUNK == 0 and seq % TILES.rnn == 0
    assert all(t % n == 0 for n in (TILES.qkv, TILES.merge, TILES.route, TILES.dense))
    assert peer_u.shape[0] == PEER_N_KEYS ** 2 and PEER_N_KEYS % TILES.dense_keys == 0

    w_in_bf = w_in.astype(BF16)
    n_grp = d // HEAD_DIM
    scale = HEAD_DIM ** -0.5
    gn = jnp.stack([jnp.tile(q_norm_g, n_grp) * (scale * LOG2_E), jnp.tile(k_norm_g, n_grp),
                    jnp.ones((d,), F32)]).reshape(3, 1, d)
    gsum = jnp.kron(jnp.eye(MXU_TILE // HEAD_DIM, dtype=F32),
                    jnp.full((HEAD_DIM, HEAD_DIM), 1.0 / HEAD_DIM, F32)).astype(BF16)
    wxy = w_in_bf[:, 3 * d:3 * d + 2 * d_rnn]
    wg = w_in_bf[:, 3 * d + 2 * d_rnn:]
    wa_bd = _block_diag_tiles(w_rg_a).astype(BF16)
    wx_bd = _block_diag_tiles(w_rg_x).astype(BF16)
    wqt = w_peer_q.T.astype(BF16)
    keys = peer_sub_keys.reshape(PEER_HEADS * 2, PEER_N_KEYS, -1).astype(BF16)
    u_bf = peer_u.astype(BF16)
    vt_bf = peer_v.T.astype(BF16)

    g1 = row(norm1_g)
    qkv = _qkv_proj(x2d, g1, w_in_bf, gn, gsum, tm=TILES.qkv)
    attn = _diff_attention(qkv, row(lambda_q1), row(lambda_k1), row(lambda_q2), row(lambda_k2),
                           row(subln_g), batch, seq, lam_init, tq=TILES.attn_q)
    rnn = _rglru(x2d, g1, wxy, conv_w, row(conv_b), wa_bd, row(b_rg_a), wx_bd, row(b_rg_x),
                 row(rg_lambda), batch, seq, ts=TILES.rnn)
    x2, xn = _merge(x2d, attn, rnn, g1, wg, row(b_gate), w_br_attn.astype(BF16), w_br_rnn.astype(BF16),
                    w_out.astype(BF16), row(norm2_g), tm=TILES.merge)
    n_w, e1_w, rank2, e2 = _peer_route(xn, wqt, keys, tm=TILES.route)
    out = _peer_dense(xn, x2, u_bf, vt_bf, n_w, e1_w, rank2, e2, tm=TILES.dense, na=TILES.dense_keys)
    return out.reshape(batch, seq, d)


def kernel(x, norm1_g, w_in, b_gate, q_norm_g, k_norm_g, lambda_q1, lambda_k1, lambda_q2, lambda_k2,
           subln_g, conv_w, conv_b, w_rg_a, b_rg_a, w_rg_x, b_rg_x, rg_lambda, w_br_attn, w_br_rnn,
           w_out, norm2_g, w_peer_q, peer_sub_keys, peer_u, peer_v):
    depth = norm1_g.shape[0]
    for layer in range(depth):
        lam_init = 0.8 - 0.6 * math.exp(-0.3 * layer)
        x = _layer(x, lam_init, norm1_g[layer], w_in[layer], b_gate[layer], q_norm_g[layer],
                   k_norm_g[layer], lambda_q1[layer], lambda_k1[layer], lambda_q2[layer],
                   lambda_k2[layer], subln_g[layer], conv_w[layer], conv_b[layer], w_rg_a[layer],
                   b_rg_a[layer], w_rg_x[layer], b_rg_x[layer], rg_lambda[layer], w_br_attn[layer],
                   w_br_rnn[layer], w_out[layer], norm2_g[layer], w_peer_q[layer], peer_sub_keys[layer],
                   peer_u[layer], peer_v[layer])
    return x
```

```python
import functools
import math
from typing import NamedTuple

import jax
import jax.numpy as jnp
from jax import lax
from jax.experimental import pallas as pl
from jax.experimental.pallas import tpu as pltpu

F32 = jnp.float32
BF16 = jnp.bfloat16

EPS = 1e-6
NEG_INF = -1e30
LOG2_E = math.log2(math.e)
CHUNK = 64
N_HEADS = 8
HEAD_DIM = 64
V_HEAD_DIM = 128
CONV_WIDTH = 4
RG_C = 8.0
PEER_HEADS = 8
PEER_N_KEYS = 128
PEER_TOPK = 16

MXU_TILE = 256
SUBLANES = 8
VMEM_V7X = 64 * 1024 * 1024
VMEM_LIMIT = VMEM_V7X * 7 // 8


class Tiles(NamedTuple):
    qkv: int = 512
    attn_q: int = 512
    rnn: int = 512
    merge: int = 512
    route: int = 256
    dense: int = 512
    dense_keys: int = 16


TILES = Tiles()


def _params(*sem):
    return pltpu.CompilerParams(dimension_semantics=sem, vmem_limit_bytes=VMEM_LIMIT)


def _rms(x, g):
    ms = jnp.mean(x * x, axis=-1, keepdims=True)
    return x * lax.rsqrt(ms + EPS) * g


def _dot(a, b):
    return jnp.dot(a, b, preferred_element_type=F32)


def _dot_nt(a, b):
    return lax.dot_general(a, b, (((1,), (1,)), ((), ())), preferred_element_type=F32)


def _qkv_kernel(x_ref, g1_ref, w_ref, gn_ref, gsum_ref, o_ref, h_ref):
    j = pl.program_id(1)

    @pl.when(j == 0)
    def _():
        h_ref[...] = _rms(x_ref[...], g1_ref[...]).astype(BF16)

    y = _dot(h_ref[...], w_ref[...])

    @pl.when(j < 2)
    def _():
        y2 = y * y
        hi = y2.astype(BF16)
        lo = (y2 - hi.astype(F32)).astype(BF16)
        g = gsum_ref[...]
        ms = jnp.concatenate(
            [_dot(hi[:, c:c + MXU_TILE], g) + _dot(lo[:, c:c + MXU_TILE], g)
             for c in range(0, y.shape[1], MXU_TILE)], axis=-1)
        o_ref[...] = (y * lax.rsqrt(ms + EPS) * gn_ref[0]).astype(BF16)

    @pl.when(j == 2)
    def _():
        o_ref[...] = y.astype(BF16)


def _qkv_proj(x2d, g1, w_in_bf, gn, gsum, tm):
    t, d = x2d.shape
    return pl.pallas_call(
        _qkv_kernel,
        out_shape=jax.ShapeDtypeStruct((t, 3 * d), BF16),
        grid=(t // tm, 3),
        in_specs=[
            pl.BlockSpec((tm, d), lambda i, j: (i, 0)),
            pl.BlockSpec((1, d), lambda i, j: (0, 0)),
            pl.BlockSpec((d, d), lambda i, j: (0, j)),
            pl.BlockSpec((1, 1, d), lambda i, j: (j, 0, 0)),
            pl.BlockSpec((MXU_TILE, MXU_TILE), lambda i, j: (0, 0)),
        ],
        out_specs=pl.BlockSpec((tm, d), lambda i, j: (i, j)),
        scratch_shapes=[pltpu.VMEM((tm, d), BF16)],
        compiler_params=_params("parallel", "arbitrary"),
        name="qkv_proj",
    )(x2d, g1, w_in_bf, gn, gsum)


def _attn_kernel(q_ref, k_ref, v_ref, lq1_ref, lk1_ref, lq2_ref, lk2_ref, sg_ref, o_ref,
                 s_buf, mrun_ref, m_ref, acc_ref, vx_ref, bias_ref, *, tq, lam_init):
    i = pl.program_id(2)
    hd = V_HEAD_DIM

    @pl.when((pl.program_id(0) == 0) & (pl.program_id(1) == 0) & (i == 0))
    def _():
        r = lax.broadcasted_iota(jnp.int32, bias_ref.shape, 0)
        c = lax.broadcasted_iota(jnp.int32, bias_ref.shape, 1)
        qpos = jnp.where(r >= tq, r - tq, r)
        bias_ref[...] = jnp.where((c // CHUNK) <= (qpos // CHUNK), 0.0, NEG_INF)

    @pl.when(i == 0)
    def _():
        vx_ref[:, :hd] = v_ref[...]
        vx_ref[:, hd:] = jnp.ones((vx_ref.shape[0], hd), BF16)

    q = q_ref[...]
    lane = lax.broadcasted_iota(jnp.int32, q.shape, 1)
    zero = jnp.zeros_like(q)
    qs = jnp.concatenate([jnp.where(lane < HEAD_DIM, q, zero), jnp.where(lane >= HEAD_DIM, q, zero)], axis=0)

    mrun_ref[...] = jnp.full(mrun_ref.shape, NEG_INF, F32)

    def scores(j, masked):
        start = pl.multiple_of(j * tq, tq)
        s = _dot_nt(qs, k_ref[pl.ds(start, tq), :])
        if masked:
            s = s + bias_ref[...]
        s_buf[j] = s
        m = mrun_ref[...]
        for c0 in range(0, tq, hd):
            m = jnp.maximum(m, s[:, c0:c0 + hd])
        mrun_ref[...] = m

    def full_scores(p, carry):
        scores(2 * p, False)
        scores(2 * p + 1, False)
        return carry

    lax.fori_loop(0, i // 2, full_scores, 0)

    @pl.when(i % 2 == 1)
    def _():
        scores(i - 1, False)

    scores(i, True)
    m_ref[...] = jnp.broadcast_to(jnp.max(mrun_ref[...], axis=-1, keepdims=True), m_ref.shape)

    acc_ref[...] = jnp.zeros(acc_ref.shape, F32)

    def weighted(j):
        start = pl.multiple_of(j * tq, tq)
        s = s_buf[j]
        m = m_ref[...]
        p = jnp.concatenate([jnp.exp2(s[:, c0:c0 + hd] - m) for c0 in range(0, tq, hd)], axis=-1)
        return _dot(p.astype(BF16), vx_ref[pl.ds(start, tq), :])

    def pair_weighted(p, carry):
        acc_ref[...] += weighted(2 * p) + weighted(2 * p + 1)
        return carry

    lax.fori_loop(0, (i + 1) // 2, pair_weighted, 0)

    @pl.when(i % 2 == 0)
    def _():
        acc_ref[...] += weighted(i)

    lam = (jnp.exp(jnp.sum(lq1_ref[...] * lk1_ref[...], axis=-1, keepdims=True))
           - jnp.exp(jnp.sum(lq2_ref[...] * lk2_ref[...], axis=-1, keepdims=True)) + lam_init)
    o = acc_ref[:, :hd] / acc_ref[:, hd:]
    o = o[:tq] - lam * o[tq:]
    o_ref[...] = (_rms(o, sg_ref[...]) * (1.0 - lam_init)).astype(BF16)


def _diff_attention(qkv, lq1, lk1, lq2, lk2, subln_g, batch, seq, lam_init, tq):
    t = qkv.shape[0]
    nq = seq // tq
    hd = V_HEAD_DIM
    small = pl.BlockSpec((1, HEAD_DIM), lambda b, h, i: (0, 0))
    return pl.pallas_call(
        functools.partial(_attn_kernel, tq=tq, lam_init=lam_init),
        out_shape=jax.ShapeDtypeStruct((t, N_HEADS * hd), BF16),
        grid=(batch, N_HEADS, nq),
        in_specs=[
            pl.BlockSpec((tq, hd), lambda b, h, i: (b * nq + i, h)),
            pl.BlockSpec((seq, hd), lambda b, h, i: (b, N_HEADS + h)),
            pl.BlockSpec((seq, hd), lambda b, h, i: (b, 2 * N_HEADS + h)),
            small, small, small, small,
            pl.BlockSpec((1, hd), lambda b, h, i: (0, 0)),
        ],
        out_specs=pl.BlockSpec((tq, hd), lambda b, h, i: (b * nq + i, h)),
        scratch_shapes=[pltpu.VMEM((nq, 2 * tq, tq), F32), pltpu.VMEM((2 * tq, hd), F32),
                        pltpu.VMEM((2 * tq, hd), F32), pltpu.VMEM((2 * tq, 2 * hd), F32),
                        pltpu.VMEM((seq, 2 * hd), BF16), pltpu.VMEM((2 * tq, tq), F32)],
        compiler_params=_params("arbitrary", "arbitrary", "arbitrary"),
        name="diff_attn",
    )(qkv, qkv, qkv, lq1, lk1, lq2, lk2, subln_g)


def _rglru_kernel(x_ref, g1_ref, wxy_ref, cw_ref, cb_ref, wa_ref, ba_ref, wx_ref, bx_ref, lam_ref, o_ref,
                  xbuf_ref, h_ref, *, ts, d_rnn):
    s_idx = pl.program_id(1)

    @pl.when(s_idx == 0)
    def _():
        xbuf_ref[0:SUBLANES, :] = jnp.zeros((SUBLANES, d_rnn), F32)
        h_ref[...] = jnp.zeros(h_ref.shape, F32)

    h_in = _rms(x_ref[...], g1_ref[...]).astype(BF16)
    xy = _dot(h_in, wxy_ref[...])
    xr = xy[:, :d_rnn]
    yr = xy[:, d_rnn:]

    xbuf_ref[SUBLANES:SUBLANES + ts, :] = xr
    conv = cb_ref[...] + jnp.zeros((ts, d_rnn), F32)
    for tap in range(CONV_WIDTH):
        off = SUBLANES - (CONV_WIDTH - 1) + tap
        conv = conv + xbuf_ref[off:off + ts, :] * cw_ref[tap:tap + 1, :]
    xbuf_ref[0:SUBLANES, :] = xbuf_ref[ts:ts + SUBLANES, :]

    cbf = conv.astype(BF16)
    ga, gi = [], []
    for g in range(d_rnn // MXU_TILE):
        sl = slice(g * MXU_TILE, (g + 1) * MXU_TILE)
        ga.append(_dot(cbf[:, sl], wa_ref[g]))
        gi.append(_dot(cbf[:, sl], wx_ref[g]))
    gate_r = jax.nn.sigmoid(jnp.concatenate(ga, axis=-1) + ba_ref[...])
    gate_i = jax.nn.sigmoid(jnp.concatenate(gi, axis=-1) + bx_ref[...])

    neg_lam = -lam_ref[...]
    softplus = jnp.maximum(neg_lam, 0.0) + jnp.log1p(jnp.exp(-jnp.abs(neg_lam)))
    log_a = -RG_C * gate_r * softplus
    a = jnp.exp(log_a)
    th = jnp.tanh(log_a)
    mult = jnp.sqrt(-2.0 * th / (1.0 - th))
    u = mult * (gate_i * conv)

    groups = ts // SUBLANES
    a3 = a.reshape(groups, SUBLANES, d_rnn)
    u3 = u.reshape(groups, SUBLANES, d_rnn)
    sub = lax.broadcasted_iota(jnp.int32, a3.shape, 1)
    d = 1
    while d < SUBLANES:
        keep = sub >= d
        a_sh = jnp.where(keep, pltpu.roll(a3, d, 1), 1.0)
        u_sh = jnp.where(keep, pltpu.roll(u3, d, 1), 0.0)
        u3 = a3 * u_sh + u3
        a3 = a3 * a_sh
        d *= 2
    carry = h_ref[...]
    hs = []
    for g in range(groups):
        hg = u3[g] + a3[g] * carry
        hs.append(hg)
        carry = hg[SUBLANES - 1:SUBLANES, :]
    h_ref[...] = carry
    o_ref[...] = (jnp.concatenate(hs, axis=0) * jax.nn.gelu(yr)).astype(BF16)


def _rglru(x2d, g1, wxy, conv_w, conv_b, wa_bd, b_a, wx_bd, b_x, rg_lambda, batch, seq, ts):
    t, d = x2d.shape
    d_rnn = conv_w.shape[1]
    ns = seq // ts
    ng = d_rnn // MXU_TILE
    const2 = lambda b, s: (0, 0)
    const3 = lambda b, s: (0, 0, 0)
    return pl.pallas_call(
        functools.partial(_rglru_kernel, ts=ts, d_rnn=d_rnn),
        out_shape=jax.ShapeDtypeStruct((t, d_rnn), BF16),
        grid=(batch, ns),
        in_specs=[
            pl.BlockSpec((ts, d), lambda b, s: (b * ns + s, 0)),
            pl.BlockSpec((1, d), const2),
            pl.BlockSpec((d, 2 * d_rnn), const2),
            pl.BlockSpec((CONV_WIDTH, d_rnn), const2),
            pl.BlockSpec((1, d_rnn), const2),
            pl.BlockSpec((ng, MXU_TILE, MXU_TILE), const3),
            pl.BlockSpec((1, d_rnn), const2),
            pl.BlockSpec((ng, MXU_TILE, MXU_TILE), const3),
            pl.BlockSpec((1, d_rnn), const2),
            pl.BlockSpec((1, d_rnn), const2),
        ],
        out_specs=pl.BlockSpec((ts, d_rnn), lambda b, s: (b * ns + s, 0)),
        scratch_shapes=[pltpu.VMEM((ts + SUBLANES, d_rnn), F32), pltpu.VMEM((1, d_rnn), F32)],
        compiler_params=_params("parallel", "arbitrary"),
        name="rglru",
    )(x2d, g1, wxy, conv_w, conv_b, wa_bd, b_a, wx_bd, b_x, rg_lambda)


def _merge_kernel(x_ref, attn_ref, rnn_ref, g1_ref, wg_ref, bg_ref, pa_ref, pr_ref, wo_ref, g2_ref,
                  x2_ref, xn_ref, *, d):
    x = x_ref[...]
    h = _rms(x, g1_ref[...]).astype(BF16)
    gates = jax.nn.sigmoid(_dot(h, wg_ref[...]) + bg_ref[...])
    merged = gates[:, :d] * _dot(attn_ref[...], pa_ref[...]) + gates[:, d:] * _dot(rnn_ref[...], pr_ref[...])
    x2 = x + _dot(merged.astype(BF16), wo_ref[...])
    x2_ref[...] = x2
    xn_ref[...] = _rms(x2, g2_ref[...]).T.astype(BF16)


def _merge(x2d, attn, rnn, g1, wg, bg, pa, pr, wo, g2, tm):
    t, d = x2d.shape
    d_rnn = rnn.shape[1]
    const = lambda i: (0, 0)
    row = lambda i: (i, 0)
    return pl.pallas_call(
        functools.partial(_merge_kernel, d=d),
        out_shape=(jax.ShapeDtypeStruct((t, d), F32), jax.ShapeDtypeStruct((d, t), BF16)),
        grid=(t // tm,),
        in_specs=[
            pl.BlockSpec((tm, d), row), pl.BlockSpec((tm, d), row), pl.BlockSpec((tm, d_rnn), row),
            pl.BlockSpec((1, d), const), pl.BlockSpec((d, 2 * d), const), pl.BlockSpec((1, 2 * d), const),
            pl.BlockSpec((d, d), const), pl.BlockSpec((d_rnn, d), const), pl.BlockSpec((d, d), const),
            pl.BlockSpec((1, d), const),
        ],
        out_specs=(pl.BlockSpec((tm, d), row), pl.BlockSpec((d, tm), lambda i: (0, i))),
        compiler_params=_params("parallel"),
        name="merge",
    )(x2d, attn, rnn, g1, wg, bg, pa, pr, wo, g2)


def _top_values(s, k):
    vals = []
    cur = s
    for r in range(k):
        m = jnp.max(cur, axis=0, keepdims=True)
        vals.append(m)
        if r + 1 < k:
            cur = jnp.where(cur == m, -jnp.inf, cur)
    return vals


def _merge_exchange_network(n):
    pairs = []
    p = 1
    while p < n:
        k = p
        while k >= 1:
            for j in range(k % p, n - k, 2 * k):
                for i in range(min(k, n - j - k)):
                    if (i + j) // (2 * p) == (i + j + k) // (2 * p):
                        pairs.append((i + j, i + j + k))
            k //= 2
        p *= 2
    return pairs


def _top_values_of_keys(s, k):
    groups = s.shape[0] // SUBLANES
    assert groups == k
    stack = [s[g * SUBLANES:(g + 1) * SUBLANES, :] for g in range(groups)]
    for i, j in _merge_exchange_network(groups):
        stack[i], stack[j] = jnp.maximum(stack[i], stack[j]), jnp.minimum(stack[i], stack[j])
    vals = []
    for r in range(k):
        m = jnp.max(stack[0], axis=0, keepdims=True)
        vals.append(m)
        if r + 1 < k:
            top = stack[0] == m
            for level in range(k - 1 - r):
                stack[level] = jnp.where(top, stack[level + 1], stack[level])
    return vals


def _first_at_least(s, v):
    assert len(v) == 16
    w = jnp.where
    c8 = s >= v[7]
    c4 = s >= w(c8, v[3], v[11])
    c2 = s >= w(c8, w(c4, v[1], v[5]), w(c4, v[9], v[13]))
    c1 = s >= w(c8, w(c4, w(c2, v[0], v[2]), w(c2, v[4], v[6])), w(c4, w(c2, v[8], v[10]), w(c2, v[12], v[14])))
    j = (w(c8, 0.0, 8.0) + w(c4, 0.0, 4.0)) + (w(c2, 0.0, 2.0) + w(c1, 0.0, 1.0))
    return w(s >= v[15], j, 16.0)


def _route_kernel(xn_ref, wqt_ref, keys_ref, n_ref, e1_ref, rank2_ref, e2_ref, *, tm):
    qt = _dot(wqt_ref[...], xn_ref[...]).astype(BF16)
    nk = PEER_N_KEYS
    for h in range(PEER_HEADS):
        s1 = _dot(keys_ref[2 * h], qt[(2 * h) * nk:(2 * h + 1) * nk, :])
        s2 = _dot(keys_ref[2 * h + 1], qt[(2 * h + 1) * nk:(2 * h + 2) * nk, :])
        v1 = _top_values_of_keys(s1, PEER_TOPK)
        v2 = _top_values_of_keys(s2, PEER_TOPK)
        rank2 = _first_at_least(s2, v2)
        cands = [v1[i] + v2[j] for i in range(PEER_TOPK) for j in range(PEER_TOPK)
                 if (i + 1) * (j + 1) <= PEER_TOPK]
        pad = (-len(cands)) % SUBLANES
        cands += [jnp.full((1, tm), -jnp.inf, F32)] * pad
        best = _top_values(jnp.concatenate(cands, axis=0), PEER_TOPK)
        tau = best[PEER_TOPK - 1]
        z = sum(jnp.exp(b - best[0]) for b in best)
        v1_rows = jnp.concatenate(v1, axis=0)
        m = sum(jnp.where(v1_rows + v2[j] >= tau, 1.0, 0.0) for j in range(PEER_TOPK))
        n = jnp.zeros(s1.shape, F32)
        for i in range(PEER_TOPK):
            n = jnp.where(s1 == v1[i], m[i:i + 1, :], n)
        n_ref[h] = n
        e1_ref[h] = 0.5 * jnp.exp(s1 - v1[0]) / z
        rank2_ref[h] = rank2.astype(BF16)
        e2_ref[h] = jnp.exp(s2 - v2[0]).astype(BF16)


def _peer_route(xn, wqt, keys, tm):
    d, t = xn.shape
    nk = PEER_N_KEYS
    words = jax.ShapeDtypeStruct((PEER_HEADS, nk, t), F32)
    halves = jax.ShapeDtypeStruct((PEER_HEADS, nk, t), BF16)
    ospec = pl.BlockSpec((PEER_HEADS, nk, tm), lambda i: (0, 0, i))
    return pl.pallas_call(
        functools.partial(_route_kernel, tm=tm),
        out_shape=(words, words, halves, halves),
        grid=(t // tm,),
        in_specs=[
            pl.BlockSpec((d, tm), lambda i: (0, i)),
            pl.BlockSpec(wqt.shape, lambda i: (0, 0)),
            pl.BlockSpec(keys.shape, lambda i: (0, 0, 0)),
        ],
        out_specs=(ospec, ospec, ospec, ospec),
        compiler_params=_params("parallel"),
        name="peer_route",
    )(xn, wqt, keys)


GELU_C0 = math.sqrt(2.0 / math.pi)
GELU_C1 = GELU_C0 * 0.044715


def _peer_dense_kernel(xn_ref, x2_ref, u_ref, vt_ref, n_ref, e1_ref, rank2_ref, e2_ref, o_ref,
                       acc_ref, s_ref, w_ref, *, na):
    j = pl.program_id(1)
    nk = PEER_N_KEYS
    tm = s_ref.shape[1]

    @pl.when(j == 0)
    def _():
        acc_ref[...] = jnp.zeros(acc_ref.shape, F32)

    s_ref[...] = _dot(u_ref[...], xn_ref[...])

    for a in range(na):
        rows = slice(a * nk, (a + 1) * nk)
        coef = None
        for h in range(PEER_HEADS):
            n_b = jnp.broadcast_to(n_ref[h, a:a + 1, :].astype(BF16), (nk, tm))
            e_b = jnp.broadcast_to(e1_ref[h, a:a + 1, :].astype(BF16), (nk, tm))
            term = e_b * jnp.where(rank2_ref[h] < n_b, e2_ref[h], jnp.zeros((), BF16))
            coef = term if coef is None else coef + term
        x = s_ref[rows, :]
        th = jnp.tanh(x * (GELU_C0 + GELU_C1 * (x * x)))
        w_ref[rows, :] = coef * (x + x * th).astype(BF16)
    acc_ref[...] += _dot(vt_ref[...], w_ref[...])

    @pl.when(j == pl.num_programs(1) - 1)
    def _():
        o_ref[...] = x2_ref[...] + acc_ref[...].T


def _peer_dense(xn, x2, u_bf, vt_bf, n_w, e1_w, rank2, e2, tm, na):
    d, t = xn.shape
    n_exp = u_bf.shape[0]
    nk = PEER_N_KEYS
    te = na * nk
    return pl.pallas_call(
        functools.partial(_peer_dense_kernel, na=na),
        out_shape=jax.ShapeDtypeStruct((t, d), F32),
        grid=(t // tm, n_exp // te),
        in_specs=[
            pl.BlockSpec((d, tm), lambda i, j: (0, i)),
            pl.BlockSpec((tm, d), lambda i, j: (i, 0)),
            pl.BlockSpec((te, d), lambda i, j: (j, 0)),
            pl.BlockSpec((d, te), lambda i, j: (0, j)),
            pl.BlockSpec((PEER_HEADS, na, tm), lambda i, j: (0, j, i)),
            pl.BlockSpec((PEER_HEADS, na, tm), lambda i, j: (0, j, i)),
            pl.BlockSpec((PEER_HEADS, nk, tm), lambda i, j: (0, 0, i)),
            pl.BlockSpec((PEER_HEADS, nk, tm), lambda i, j: (0, 0, i)),
        ],
        out_specs=pl.BlockSpec((tm, d), lambda i, j: (i, 0)),
        scratch_shapes=[pltpu.VMEM((d, tm), F32), pltpu.VMEM((te, tm), F32), pltpu.VMEM((te, tm), BF16)],
        compiler_params=_params("parallel", "arbitrary"),
        name="peer_dense",
    )(xn, x2, u_bf, vt_bf, n_w, e1_w, rank2, e2)


def _block_diag_tiles(w):
    nb, r, _ = w.shape
    per = MXU_TILE // r
    w = w.reshape(nb // per, per, r, r)
    eye = jnp.eye(per, dtype=w.dtype)
    return jnp.einsum("gpij,pq->gpiqj", w, eye).reshape(nb // per, MXU_TILE, MXU_TILE)


def _layer(x, lam_init, norm1_g, w_in, b_gate, q_norm_g, k_norm_g, lambda_q1, lambda_k1, lambda_q2,
           lambda_k2, subln_g, conv_w, conv_b, w_rg_a, b_rg_a, w_rg_x, b_rg_x, rg_lambda, w_br_attn,
           w_br_rnn, w_out, norm2_g, w_peer_q, peer_sub_keys, peer_u, peer_v):
    batch, seq, d = x.shape
    t = batch * seq
    d_rnn = conv_w.shape[1]
    x2d = x.reshape(t, d)
    row = lambda v: v.reshape(1, -1)
    assert d == N_HEADS * V_HEAD_DIM and d_rnn % MXU_TILE == 0 and MXU_TILE % w_rg_a.shape[1] == 0
    assert seq % TILES.attn_q == 0 and TILES.attn_q % CHUNK == 0 and seq % TILES.rnn == 0
    assert all(t % n == 0 for n in (TILES.qkv, TILES.merge, TILES.route, TILES.dense))
    assert peer_u.shape[0] == PEER_N_KEYS ** 2 and PEER_N_KEYS % TILES.dense_keys == 0

    w_in_bf = w_in.astype(BF16)
    n_grp = d // HEAD_DIM
    scale = HEAD_DIM ** -0.5
    gn = jnp.stack([jnp.tile(q_norm_g, n_grp) * (scale * LOG2_E), jnp.tile(k_norm_g, n_grp),
                    jnp.ones((d,), F32)]).reshape(3, 1, d)
    gsum = jnp.kron(jnp.eye(MXU_TILE // HEAD_DIM, dtype=F32),
                    jnp.full((HEAD_DIM, HEAD_DIM), 1.0 / HEAD_DIM, F32)).astype(BF16)
    wxy = w_in_bf[:, 3 * d:3 * d + 2 * d_rnn]
    wg = w_in_bf[:, 3 * d + 2 * d_rnn:]
    wa_bd = _block_diag_tiles(w_rg_a).astype(BF16)
    wx_bd = _block_diag_tiles(w_rg_x).astype(BF16)
    wqt = w_peer_q.T.astype(BF16)
    keys = peer_sub_keys.reshape(PEER_HEADS * 2, PEER_N_KEYS, -1).astype(BF16)
    u_bf = peer_u.astype(BF16)
    vt_bf = peer_v.T.astype(BF16)

    g1 = row(norm1_g)
    qkv = _qkv_proj(x2d, g1, w_in_bf, gn, gsum, tm=TILES.qkv)
    attn = _diff_attention(qkv, row(lambda_q1), row(lambda_k1), row(lambda_q2), row(lambda_k2),
                           row(subln_g), batch, seq, lam_init, tq=TILES.attn_q)
    rnn = _rglru(x2d, g1, wxy, conv_w, row(conv_b), wa_bd, row(b_rg_a), wx_bd, row(b_rg_x),
                 row(rg_lambda), batch, seq, ts=TILES.rnn)
    x2, xn = _merge(x2d, attn, rnn, g1, wg, row(b_gate), w_br_attn.astype(BF16), w_br_rnn.astype(BF16),
                    w_out.astype(BF16), row(norm2_g), tm=TILES.merge)
    n_w, e1_w, rank2, e2 = _peer_route(xn, wqt, keys, tm=TILES.route)
    out = _peer_dense(xn, x2, u_bf, vt_bf, n_w, e1_w, rank2, e2, tm=TILES.dense, na=TILES.dense_keys)
    return out.reshape(batch, seq, d)


def kernel(x, norm1_g, w_in, b_gate, q_norm_g, k_norm_g, lambda_q1, lambda_k1, lambda_q2, lambda_k2,
           subln_g, conv_w, conv_b, w_rg_a, b_rg_a, w_rg_x, b_rg_x, rg_lambda, w_br_attn, w_br_rnn,
           w_out, norm2_g, w_peer_q, peer_sub_keys, peer_u, peer_v):
    depth = norm1_g.shape[0]
    for layer in range(depth):
        lam_init = 0.8 - 0.6 * math.exp(-0.3 * layer)
        x = _layer(x, lam_init, norm1_g[layer], w_in[layer], b_gate[layer], q_norm_g[layer],
                   k_norm_g[layer], lambda_q1[layer], lambda_k1[layer], lambda_q2[layer],
                   lambda_k2[layer], subln_g[layer], conv_w[layer], conv_b[layer], w_rg_a[layer],
                   b_rg_a[layer], w_rg_x[layer], b_rg_x[layer], rg_lambda[layer], w_br_attn[layer],
                   w_br_rnn[layer], w_out[layer], norm2_g[layer], w_peer_q[layer], peer_sub_keys[layer],
                   peer_u[layer], peer_v[layer])
    return x
```

```python
import functools
import math
from typing import NamedTuple

import jax
import jax.numpy as jnp
from jax import lax
from jax.experimental import pallas as pl
from jax.experimental.pallas import tpu as pltpu

F32 = jnp.float32
BF16 = jnp.bfloat16

EPS = 1e-6
NEG_INF = -1e30
LOG2_E = math.log2(math.e)
CHUNK = 64
N_HEADS = 8
HEAD_DIM = 64
V_HEAD_DIM = 128
CONV_WIDTH = 4
RG_C = 8.0
PEER_HEADS = 8
PEER_N_KEYS = 128
PEER_TOPK = 16

MXU_TILE = 256
SUBLANES = 8
VMEM_V7X = 64 * 1024 * 1024
VMEM_LIMIT = VMEM_V7X * 7 // 8


class Tiles(NamedTuple):
    qkv: int = 1024
    attn_q: int = 512
    rnn: int = 512
    merge: int = 512
    route: int = 256
    dense: int = 512
    dense_keys: int = 16


TILES = Tiles()


def _params(*sem):
    return pltpu.CompilerParams(dimension_semantics=sem, vmem_limit_bytes=VMEM_LIMIT)


def _rms(x, g):
    ms = jnp.mean(x * x, axis=-1, keepdims=True)
    return x * lax.rsqrt(ms + EPS) * g


def _dot(a, b):
    return jnp.dot(a, b, preferred_element_type=F32)


def _dot_nt(a, b):
    return lax.dot_general(a, b, (((1,), (1,)), ((), ())), preferred_element_type=F32)


def _qkv_kernel(x_ref, g1_ref, w_ref, gn_ref, gsum_ref, o_ref, h_ref):
    j = pl.program_id(1)

    @pl.when(j == 0)
    def _():
        h_ref[...] = _rms(x_ref[...], g1_ref[...]).astype(BF16)

    y = _dot(h_ref[...], w_ref[...])

    @pl.when(j < 2)
    def _():
        y2 = y * y
        hi = y2.astype(BF16)
        lo = (y2 - hi.astype(F32)).astype(BF16)
        g = gsum_ref[...]
        ms = jnp.concatenate(
            [_dot(hi[:, c:c + MXU_TILE], g) + _dot(lo[:, c:c + MXU_TILE], g)
             for c in range(0, y.shape[1], MXU_TILE)], axis=-1)
        o_ref[...] = (y * lax.rsqrt(ms + EPS) * gn_ref[0]).astype(BF16)

    @pl.when(j == 2)
    def _():
        o_ref[...] = y.astype(BF16)


def _qkv_proj(x2d, g1, w_in_bf, gn, gsum, tm):
    t, d = x2d.shape
    return pl.pallas_call(
        _qkv_kernel,
        out_shape=jax.ShapeDtypeStruct((t, 3 * d), BF16),
        grid=(t // tm, 3),
        in_specs=[
            pl.BlockSpec((tm, d), lambda i, j: (i, 0)),
            pl.BlockSpec((1, d), lambda i, j: (0, 0)),
            pl.BlockSpec((d, d), lambda i, j: (0, j)),
            pl.BlockSpec((1, 1, d), lambda i, j: (j, 0, 0)),
            pl.BlockSpec((MXU_TILE, MXU_TILE), lambda i, j: (0, 0)),
        ],
        out_specs=pl.BlockSpec((tm, d), lambda i, j: (i, j)),
        scratch_shapes=[pltpu.VMEM((tm, d), BF16)],
        compiler_params=_params("parallel", "arbitrary"),
        name="qkv_proj",
    )(x2d, g1, w_in_bf, gn, gsum)


def _attn_kernel(q_ref, k_ref, v_ref, lq1_ref, lk1_ref, lq2_ref, lk2_ref, sg_ref, o_ref,
                 s_buf, mrun_ref, m_ref, acc_ref, vx_ref, bias_ref, *, tq, lam_init):
    i = pl.program_id(2)
    hd = V_HEAD_DIM

    @pl.when((pl.program_id(0) == 0) & (pl.program_id(1) == 0) & (i == 0))
    def _():
        r = lax.broadcasted_iota(jnp.int32, bias_ref.shape, 0)
        c = lax.broadcasted_iota(jnp.int32, bias_ref.shape, 1)
        qpos = jnp.where(r >= tq, r - tq, r)
        bias_ref[...] = jnp.where((c // CHUNK) <= (qpos // CHUNK), 0.0, NEG_INF)

    @pl.when(i == 0)
    def _():
        vx_ref[:, :hd] = v_ref[...]
        vx_ref[:, hd:] = jnp.ones((vx_ref.shape[0], hd), BF16)

    q = q_ref[...]
    lane = lax.broadcasted_iota(jnp.int32, q.shape, 1)
    zero = jnp.zeros_like(q)
    qs = jnp.concatenate([jnp.where(lane < HEAD_DIM, q, zero), jnp.where(lane >= HEAD_DIM, q, zero)], axis=0)

    mrun_ref[...] = jnp.full(mrun_ref.shape, NEG_INF, F32)

    def scores(j, masked):
        start = pl.multiple_of(j * tq, tq)
        s = _dot_nt(qs, k_ref[pl.ds(start, tq), :])
        if masked:
            s = s + bias_ref[...]
        s_buf[j] = s
        m = mrun_ref[...]
        for c0 in range(0, tq, hd):
            m = jnp.maximum(m, s[:, c0:c0 + hd])
        mrun_ref[...] = m

    def full_scores(p, carry):
        scores(2 * p, False)
        scores(2 * p + 1, False)
        return carry

    lax.fori_loop(0, i // 2, full_scores, 0)

    @pl.when(i % 2 == 1)
    def _():
        scores(i - 1, False)

    scores(i, True)
    m_ref[...] = jnp.broadcast_to(jnp.max(mrun_ref[...], axis=-1, keepdims=True), m_ref.shape)

    acc_ref[...] = jnp.zeros(acc_ref.shape, F32)

    def weighted(j):
        start = pl.multiple_of(j * tq, tq)
        s = s_buf[j]
        m = m_ref[...]
        p = jnp.concatenate([jnp.exp2(s[:, c0:c0 + hd] - m) for c0 in range(0, tq, hd)], axis=-1)
        return _dot(p.astype(BF16), vx_ref[pl.ds(start, tq), :])

    def pair_weighted(p, carry):
        acc_ref[...] += weighted(2 * p) + weighted(2 * p + 1)
        return carry

    lax.fori_loop(0, (i + 1) // 2, pair_weighted, 0)

    @pl.when(i % 2 == 0)
    def _():
        acc_ref[...] += weighted(i)

    lam = (jnp.exp(jnp.sum(lq1_ref[...] * lk1_ref[...], axis=-1, keepdims=True))
           - jnp.exp(jnp.sum(lq2_ref[...] * lk2_ref[...], axis=-1, keepdims=True)) + lam_init)
    o = acc_ref[:, :hd] / acc_ref[:, hd:]
    o = o[:tq] - lam * o[tq:]
    o_ref[...] = (_rms(o, sg_ref[...]) * (1.0 - lam_init)).astype(BF16)


def _diff_attention(qkv, lq1, lk1, lq2, lk2, subln_g, batch, seq, lam_init, tq):
    t = qkv.shape[0]
    nq = seq // tq
    hd = V_HEAD_DIM
    small = pl.BlockSpec((1, HEAD_DIM), lambda b, h, i: (0, 0))
    return pl.pallas_call(
        functools.partial(_attn_kernel, tq=tq, lam_init=lam_init),
        out_shape=jax.ShapeDtypeStruct((t, N_HEADS * hd), BF16),
        grid=(batch, N_HEADS, nq),
        in_specs=[
            pl.BlockSpec((tq, hd), lambda b, h, i: (b * nq + i, h)),
            pl.BlockSpec((seq, hd), lambda b, h, i: (b, N_HEADS + h)),
            pl.BlockSpec((seq, hd), lambda b, h, i: (b, 2 * N_HEADS + h)),
            small, small, small, small,
            pl.BlockSpec((1, hd), lambda b, h, i: (0, 0)),
        ],
        out_specs=pl.BlockSpec((tq, hd), lambda b, h, i: (b * nq + i, h)),
        scratch_shapes=[pltpu.VMEM((nq, 2 * tq, tq), F32), pltpu.VMEM((2 * tq, hd), F32),
                        pltpu.VMEM((2 * tq, hd), F32), pltpu.VMEM((2 * tq, 2 * hd), F32),
                        pltpu.VMEM((seq, 2 * hd), BF16), pltpu.VMEM((2 * tq, tq), F32)],
        compiler_params=_params("arbitrary", "arbitrary", "arbitrary"),
        name="diff_attn",
    )(qkv, qkv, qkv, lq1, lk1, lq2, lk2, subln_g)


def _rglru_kernel(x_ref, g1_ref, wxy_ref, cw_ref, cb_ref, wa_ref, ba_ref, wx_ref, bx_ref, lam_ref, o_ref,
                  xbuf_ref, h_ref, *, ts, d_rnn):
    s_idx = pl.program_id(1)

    @pl.when(s_idx == 0)
    def _():
        xbuf_ref[0:SUBLANES, :] = jnp.zeros((SUBLANES, d_rnn), F32)
        h_ref[...] = jnp.zeros(h_ref.shape, F32)

    h_in = _rms(x_ref[...], g1_ref[...]).astype(BF16)
    xy = _dot(h_in, wxy_ref[...])
    xr = xy[:, :d_rnn]
    yr = xy[:, d_rnn:]

    xbuf_ref[SUBLANES:SUBLANES + ts, :] = xr
    conv = cb_ref[...] + jnp.zeros((ts, d_rnn), F32)
    for tap in range(CONV_WIDTH):
        off = SUBLANES - (CONV_WIDTH - 1) + tap
        conv = conv + xbuf_ref[off:off + ts, :] * cw_ref[tap:tap + 1, :]
    xbuf_ref[0:SUBLANES, :] = xbuf_ref[ts:ts + SUBLANES, :]

    cbf = conv.astype(BF16)
    ga, gi = [], []
    for g in range(d_rnn // MXU_TILE):
        sl = slice(g * MXU_TILE, (g + 1) * MXU_TILE)
        ga.append(_dot(cbf[:, sl], wa_ref[g]))
        gi.append(_dot(cbf[:, sl], wx_ref[g]))
    gate_r = jax.nn.sigmoid(jnp.concatenate(ga, axis=-1) + ba_ref[...])
    gate_i = jax.nn.sigmoid(jnp.concatenate(gi, axis=-1) + bx_ref[...])

    neg_lam = -lam_ref[...]
    softplus = jnp.maximum(neg_lam, 0.0) + jnp.log1p(jnp.exp(-jnp.abs(neg_lam)))
    log_a = -RG_C * gate_r * softplus
    a = jnp.exp(log_a)
    th = jnp.tanh(log_a)
    mult = jnp.sqrt(-2.0 * th / (1.0 - th))
    u = mult * (gate_i * conv)

    groups = ts // SUBLANES
    a3 = a.reshape(groups, SUBLANES, d_rnn)
    u3 = u.reshape(groups, SUBLANES, d_rnn)
    sub = lax.broadcasted_iota(jnp.int32, a3.shape, 1)
    d = 1
    while d < SUBLANES:
        keep = sub >= d
        a_sh = jnp.where(keep, pltpu.roll(a3, d, 1), 1.0)
        u_sh = jnp.where(keep, pltpu.roll(u3, d, 1), 0.0)
        u3 = a3 * u_sh + u3
        a3 = a3 * a_sh
        d *= 2
    carry = h_ref[...]
    hs = []
    for g in range(groups):
        hg = u3[g] + a3[g] * carry
        hs.append(hg)
        carry = hg[SUBLANES - 1:SUBLANES, :]
    h_ref[...] = carry
    o_ref[...] = (jnp.concatenate(hs, axis=0) * jax.nn.gelu(yr)).astype(BF16)


def _rglru(x2d, g1, wxy, conv_w, conv_b, wa_bd, b_a, wx_bd, b_x, rg_lambda, batch, seq, ts):
    t, d = x2d.shape
    d_rnn = conv_w.shape[1]
    ns = seq // ts
    ng = d_rnn // MXU_TILE
    const2 = lambda b, s: (0, 0)
    const3 = lambda b, s: (0, 0, 0)
    return pl.pallas_call(
        functools.partial(_rglru_kernel, ts=ts, d_rnn=d_rnn),
        out_shape=jax.ShapeDtypeStruct((t, d_rnn), BF16),
        grid=(batch, ns),
        in_specs=[
            pl.BlockSpec((ts, d), lambda b, s: (b * ns + s, 0)),
            pl.BlockSpec((1, d), const2),
            pl.BlockSpec((d, 2 * d_rnn), const2),
            pl.BlockSpec((CONV_WIDTH, d_rnn), const2),
            pl.BlockSpec((1, d_rnn), const2),
            pl.BlockSpec((ng, MXU_TILE, MXU_TILE), const3),
            pl.BlockSpec((1, d_rnn), const2),
            pl.BlockSpec((ng, MXU_TILE, MXU_TILE), const3),
            pl.BlockSpec((1, d_rnn), const2),
            pl.BlockSpec((1, d_rnn), const2),
        ],
        out_specs=pl.BlockSpec((ts, d_rnn), lambda b, s: (b * ns + s, 0)),
        scratch_shapes=[pltpu.VMEM((ts + SUBLANES, d_rnn), F32), pltpu.VMEM((1, d_rnn), F32)],
        compiler_params=_params("parallel", "arbitrary"),
        name="rglru",
    )(x2d, g1, wxy, conv_w, conv_b, wa_bd, b_a, wx_bd, b_x, rg_lambda)


def _merge_kernel(x_ref, attn_ref, rnn_ref, g1_ref, wg_ref, bg_ref, pa_ref, pr_ref, wo_ref, g2_ref,
                  x2_ref, xn_ref, *, d):
    x = x_ref[...]
    h = _rms(x, g1_ref[...]).astype(BF16)
    gates = jax.nn.sigmoid(_dot(h, wg_ref[...]) + bg_ref[...])
    merged = gates[:, :d] * _dot(attn_ref[...], pa_ref[...]) + gates[:, d:] * _dot(rnn_ref[...], pr_ref[...])
    x2 = x + _dot(merged.astype(BF16), wo_ref[...])
    x2_ref[...] = x2
    xn_ref[...] = _rms(x2, g2_ref[...]).T.astype(BF16)


def _merge(x2d, attn, rnn, g1, wg, bg, pa, pr, wo, g2, tm):
    t, d = x2d.shape
    d_rnn = rnn.shape[1]
    const = lambda i: (0, 0)
    row = lambda i: (i, 0)
    return pl.pallas_call(
        functools.partial(_merge_kernel, d=d),
        out_shape=(jax.ShapeDtypeStruct((t, d), F32), jax.ShapeDtypeStruct((d, t), BF16)),
        grid=(t // tm,),
        in_specs=[
            pl.BlockSpec((tm, d), row), pl.BlockSpec((tm, d), row), pl.BlockSpec((tm, d_rnn), row),
            pl.BlockSpec((1, d), const), pl.BlockSpec((d, 2 * d), const), pl.BlockSpec((1, 2 * d), const),
            pl.BlockSpec((d, d), const), pl.BlockSpec((d_rnn, d), const), pl.BlockSpec((d, d), const),
            pl.BlockSpec((1, d), const),
        ],
        out_specs=(pl.BlockSpec((tm, d), row), pl.BlockSpec((d, tm), lambda i: (0, i))),
        compiler_params=_params("parallel"),
        name="merge",
    )(x2d, attn, rnn, g1, wg, bg, pa, pr, wo, g2)


def _top_values(s, k):
    vals = []
    cur = s
    for r in range(k):
        m = jnp.max(cur, axis=0, keepdims=True)
        vals.append(m)
        if r + 1 < k:
            cur = jnp.where(cur == m, -jnp.inf, cur)
    return vals


def _merge_exchange_network(n):
    pairs = []
    p = 1
    while p < n:
        k = p
        while k >= 1:
            for j in range(k % p, n - k, 2 * k):
                for i in range(min(k, n - j - k)):
                    if (i + j) // (2 * p) == (i + j + k) // (2 * p):
                        pairs.append((i + j, i + j + k))
            k //= 2
        p *= 2
    return pairs


def _top_values_of_keys(s, k):
    groups = s.shape[0] // SUBLANES
    assert groups == k
    stack = [s[g * SUBLANES:(g + 1) * SUBLANES, :] for g in range(groups)]
    for i, j in _merge_exchange_network(groups):
        stack[i], stack[j] = jnp.maximum(stack[i], stack[j]), jnp.minimum(stack[i], stack[j])
    vals = []
    for r in range(k):
        m = jnp.max(stack[0], axis=0, keepdims=True)
        vals.append(m)
        if r + 1 < k:
            top = stack[0] == m
            for level in range(k - 1 - r):
                stack[level] = jnp.where(top, stack[level + 1], stack[level])
    return vals


def _first_at_least(s, v):
    assert len(v) == 16
    w = jnp.where
    c8 = s >= v[7]
    c4 = s >= w(c8, v[3], v[11])
    c2 = s >= w(c8, w(c4, v[1], v[5]), w(c4, v[9], v[13]))
    c1 = s >= w(c8, w(c4, w(c2, v[0], v[2]), w(c2, v[4], v[6])), w(c4, w(c2, v[8], v[10]), w(c2, v[12], v[14])))
    j = (w(c8, 0.0, 8.0) + w(c4, 0.0, 4.0)) + (w(c2, 0.0, 2.0) + w(c1, 0.0, 1.0))
    return w(s >= v[15], j, 16.0)


def _route_kernel(xn_ref, wqt_ref, keys_ref, n_ref, e1_ref, rank2_ref, e2_ref, *, tm):
    qt = _dot(wqt_ref[...], xn_ref[...]).astype(BF16)
    nk = PEER_N_KEYS
    for h in range(PEER_HEADS):
        s1 = _dot(keys_ref[2 * h], qt[(2 * h) * nk:(2 * h + 1) * nk, :])
        s2 = _dot(keys_ref[2 * h + 1], qt[(2 * h + 1) * nk:(2 * h + 2) * nk, :])
        v1 = _top_values_of_keys(s1, PEER_TOPK)
        v2 = _top_values_of_keys(s2, PEER_TOPK)
        rank2 = _first_at_least(s2, v2)
        cands = [v1[i] + v2[j] for i in range(PEER_TOPK) for j in range(PEER_TOPK)
                 if (i + 1) * (j + 1) <= PEER_TOPK]
        pad = (-len(cands)) % SUBLANES
        cands += [jnp.full((1, tm), -jnp.inf, F32)] * pad
        best = _top_values(jnp.concatenate(cands, axis=0), PEER_TOPK)
        tau = best[PEER_TOPK - 1]
        z = sum(jnp.exp(b - best[0]) for b in best)
        v1_rows = jnp.concatenate(v1, axis=0)
        m = sum(jnp.where(v1_rows + v2[j] >= tau, 1.0, 0.0) for j in range(PEER_TOPK))
        n = jnp.zeros(s1.shape, F32)
        for i in range(PEER_TOPK):
            n = jnp.where(s1 == v1[i], m[i:i + 1, :], n)
        n_ref[h] = n
        e1_ref[h] = 0.5 * jnp.exp(s1 - v1[0]) / z
        rank2_ref[h] = rank2.astype(BF16)
        e2_ref[h] = jnp.exp(s2 - v2[0]).astype(BF16)


def _peer_route(xn, wqt, keys, tm):
    d, t = xn.shape
    nk = PEER_N_KEYS
    words = jax.ShapeDtypeStruct((PEER_HEADS, nk, t), F32)
    halves = jax.ShapeDtypeStruct((PEER_HEADS, nk, t), BF16)
    ospec = pl.BlockSpec((PEER_HEADS, nk, tm), lambda i: (0, 0, i))
    return pl.pallas_call(
        functools.partial(_route_kernel, tm=tm),
        out_shape=(words, words, halves, halves),
        grid=(t // tm,),
        in_specs=[
            pl.BlockSpec((d, tm), lambda i: (0, i)),
            pl.BlockSpec(wqt.shape, lambda i: (0, 0)),
            pl.BlockSpec(keys.shape, lambda i: (0, 0, 0)),
        ],
        out_specs=(ospec, ospec, ospec, ospec),
        compiler_params=_params("parallel"),
        name="peer_route",
    )(xn, wqt, keys)


GELU_C0 = math.sqrt(2.0 / math.pi)
GELU_C1 = GELU_C0 * 0.044715


def _peer_dense_kernel(xn_ref, x2_ref, u_ref, vt_ref, n_ref, e1_ref, rank2_ref, e2_ref, o_ref,
                       acc_ref, s_ref, w_ref, *, na):
    j = pl.program_id(1)
    nk = PEER_N_KEYS
    tm = s_ref.shape[1]

    @pl.when(j == 0)
    def _():
        acc_ref[...] = jnp.zeros(acc_ref.shape, F32)

    s_ref[...] = _dot(u_ref[...], xn_ref[...])

    for a in range(na):
        rows = slice(a * nk, (a + 1) * nk)
        coef = None
        for h in range(PEER_HEADS):
            n_b = jnp.broadcast_to(n_ref[h, a:a + 1, :].astype(BF16), (nk, tm))
            e_b = jnp.broadcast_to(e1_ref[h, a:a + 1, :].astype(BF16), (nk, tm))
            term = e_b * jnp.where(rank2_ref[h] < n_b, e2_ref[h], jnp.zeros((), BF16))
            coef = term if coef is None else coef + term
        x = s_ref[rows, :]
        th = jnp.tanh(x * (GELU_C0 + GELU_C1 * (x * x)))
        w_ref[rows, :] = coef * (x + x * th).astype(BF16)
    acc_ref[...] += _dot(vt_ref[...], w_ref[...])

    @pl.when(j == pl.num_programs(1) - 1)
    def _():
        o_ref[...] = x2_ref[...] + acc_ref[...].T


def _peer_dense(xn, x2, u_bf, vt_bf, n_w, e1_w, rank2, e2, tm, na):
    d, t = xn.shape
    n_exp = u_bf.shape[0]
    nk = PEER_N_KEYS
    te = na * nk
    return pl.pallas_call(
        functools.partial(_peer_dense_kernel, na=na),
        out_shape=jax.ShapeDtypeStruct((t, d), F32),
        grid=(t // tm, n_exp // te),
        in_specs=[
            pl.BlockSpec((d, tm), lambda i, j: (0, i)),
            pl.BlockSpec((tm, d), lambda i, j: (i, 0)),
            pl.BlockSpec((te, d), lambda i, j: (j, 0)),
            pl.BlockSpec((d, te), lambda i, j: (0, j)),
            pl.BlockSpec((PEER_HEADS, na, tm), lambda i, j: (0, j, i)),
            pl.BlockSpec((PEER_HEADS, na, tm), lambda i, j: (0, j, i)),
            pl.BlockSpec((PEER_HEADS, nk, tm), lambda i, j: (0, 0, i)),
            pl.BlockSpec((PEER_HEADS, nk, tm), lambda i, j: (0, 0, i)),
        ],
        out_specs=pl.BlockSpec((tm, d), lambda i, j: (i, 0)),
        scratch_shapes=[pltpu.VMEM((d, tm), F32), pltpu.VMEM((te, tm), F32), pltpu.VMEM((te, tm), BF16)],
        compiler_params=_params("parallel", "arbitrary"),
        name="peer_dense",
    )(xn, x2, u_bf, vt_bf, n_w, e1_w, rank2, e2)


def _block_diag_tiles(w):
    nb, r, _ = w.shape
    per = MXU_TILE // r
    w = w.reshape(nb // per, per, r, r)
    eye = jnp.eye(per, dtype=w.dtype)
    return jnp.einsum("gpij,pq->gpiqj", w, eye).reshape(nb // per, MXU_TILE, MXU_TILE)


def _layer(x, lam_init, norm1_g, w_in, b_gate, q_norm_g, k_norm_g, lambda_q1, lambda_k1, lambda_q2,
           lambda_k2, subln_g, conv_w, conv_b, w_rg_a, b_rg_a, w_rg_x, b_rg_x, rg_lambda, w_br_attn,
           w_br_rnn, w_out, norm2_g, w_peer_q, peer_sub_keys, peer_u, peer_v):
    batch, seq, d = x.shape
    t = batch * seq
    d_rnn = conv_w.shape[1]
    x2d = x.reshape(t, d)
    row = lambda v: v.reshape(1, -1)
    assert d == N_HEADS * V_HEAD_DIM and d_rnn % MXU_TILE == 0 and MXU_TILE % w_rg_a.shape[1] == 0
    assert seq % TILES.attn_q == 0 and TILES.attn_q % CHUNK == 0 and seq % TILES.rnn == 0
    assert all(t % n == 0 for n in (TILES.qkv, TILES.merge, TILES.route, TILES.dense))
    assert peer_u.shape[0] == PEER_N_KEYS ** 2 and PEER_N_KEYS % TILES.dense_keys == 0

    w_in_bf = w_in.astype(BF16)
    n_grp = d // HEAD_DIM
    scale = HEAD_DIM ** -0.5
    gn = jnp.stack([jnp.tile(q_norm_g, n_grp) * (scale * LOG2_E), jnp.tile(k_norm_g, n_grp),
                    jnp.ones((d,), F32)]).reshape(3, 1, d)
    gsum = jnp.kron(jnp.eye(MXU_TILE // HEAD_DIM, dtype=F32),
                    jnp.full((HEAD_DIM, HEAD_DIM), 1.0 / HEAD_DIM, F32)).astype(BF16)
    wxy = w_in_bf[:, 3 * d:3 * d + 2 * d_rnn]
    wg = w_in_bf[:, 3 * d + 2 * d_rnn:]
    wa_bd = _block_diag_tiles(w_rg_a).astype(BF16)
    wx_bd = _block_diag_tiles(w_rg_x).astype(BF16)
    wqt = w_peer_q.T.astype(BF16)
    keys = peer_sub_keys.reshape(PEER_HEADS * 2, PEER_N_KEYS, -1).astype(BF16)
    u_bf = peer_u.astype(BF16)
    vt_bf = peer_v.T.astype(BF16)

    g1 = row(norm1_g)
    qkv = _qkv_proj(x2d, g1, w_in_bf, gn, gsum, tm=TILES.qkv)
    attn = _diff_attention(qkv, row(lambda_q1), row(lambda_k1), row(lambda_q2), row(lambda_k2),
                           row(subln_g), batch, seq, lam_init, tq=TILES.attn_q)
    rnn = _rglru(x2d, g1, wxy, conv_w, row(conv_b), wa_bd, row(b_rg_a), wx_bd, row(b_rg_x),
                 row(rg_lambda), batch, seq, ts=TILES.rnn)
    x2, xn = _merge(x2d, attn, rnn, g1, wg, row(b_gate), w_br_attn.astype(BF16), w_br_rnn.astype(BF16),
                    w_out.astype(BF16), row(norm2_g), tm=TILES.merge)
    n_w, e1_w, rank2, e2 = _peer_route(xn, wqt, keys, tm=TILES.route)
    out = _peer_dense(xn, x2, u_bf, vt_bf, n_w, e1_w, rank2, e2, tm=TILES.dense, na=TILES.dense_keys)
    return out.reshape(batch, seq, d)


def kernel(x, norm1_g, w_in, b_gate, q_norm_g, k_norm_g, lambda_q1, lambda_k1, lambda_q2, lambda_k2,
           subln_g, conv_w, conv_b, w_rg_a, b_rg_a, w_rg_x, b_rg_x, rg_lambda, w_br_attn, w_br_rnn,
           w_out, norm2_g, w_peer_q, peer_sub_keys, peer_u, peer_v):
    depth = norm1_g.shape[0]
    for layer in range(depth):
        lam_init = 0.8 - 0.6 * math.exp(-0.3 * layer)
        x = _layer(x, lam_init, norm1_g[layer], w_in[layer], b_gate[layer], q_norm_g[layer],
                   k_norm_g[layer], lambda_q1[layer], lambda_k1[layer], lambda_q2[layer],
                   lambda_k2[layer], subln_g[layer], conv_w[layer], conv_b[layer], w_rg_a[layer],
                   b_rg_a[layer], w_rg_x[layer], b_rg_x[layer], rg_lambda[layer], w_br_attn[layer],
                   w_br_rnn[layer], w_out[layer], norm2_g[layer], w_peer_q[layer], peer_sub_keys[layer],
                   peer_u[layer], peer_v[layer])
    return x
```
